```python
import jax, jax.numpy as jnp
from jax import lax
import numpy as np

D_MODEL = 1024
BATCH = 16
SEQ = 4096
DEPTH = 2

CTX_LEN = 256
GRID_W = 64
N_MOD = 9
D_FF = 2816
MIX_W = D_MODEL
GLA_HEADS = 4
GLA_DV = MIX_W // 2
GLA_DK = GLA_DV // 2
GLA_HEAD_V = GLA_DV // GLA_HEADS
GLA_HEAD_K = GLA_DK // GLA_HEADS
GLA_LOWRANK = 16
GATE_NORM = 16.0
GLA_CHUNK = 64
POOL_W = MIX_W // 4
POOL_WINDOWS = (2, 4, 8, 16)
POOL_GROUP = POOL_W // len(POOL_WINDOWS)
CONV_W = MIX_W // 4
CONV_K = 31
EPS = 1e-6
SPLIT_SIZES = (GLA_DK, GLA_DK, GLA_DV, GLA_DV, GLA_LOWRANK, GLA_LOWRANK, POOL_W, 2 * CONV_W)
D_IN = sum(SPLIT_SIZES)
SPLIT_IDX = [int(i) for i in np.cumsum(SPLIT_SIZES)[:-1]]

kernel_name = "hybrid_gla_pool_conv_macaron_dit"


def rms_norm(x, g):
    xf = x.astype(jnp.float32)
    y = xf * lax.rsqrt(jnp.mean(xf * xf, axis=-1, keepdims=True) + EPS)
    return (y * g.astype(jnp.float32)).astype(x.dtype)


def layer_norm(x, g, b):
    xf = x.astype(jnp.float32)
    mu = jnp.mean(xf, axis=-1, keepdims=True)
    var = jnp.mean(jnp.square(xf - mu), axis=-1, keepdims=True)
    y = (xf - mu) * lax.rsqrt(var + EPS) * g.astype(jnp.float32) + b.astype(jnp.float32)
    return y.astype(x.dtype)


def swiglu_ffn(h, w_up, w_down):
    a, b = jnp.split(h @ w_up, 2, axis=-1)
    return (jax.nn.silu(a) * b) @ w_down


def sub_in(h, gains, mods, i):
    return rms_norm(h, gains[2 * i]) * (1.0 + mods[3 * i + 1]) + mods[3 * i]


def sub_out(h, y, gains, mods, i, weight):
    return h + weight * mods[3 * i + 2] * rms_norm(y, gains[2 * i + 1])


def box_mean(x, axis, w):
    n = x.shape[axis]
    lo = w // 2
    hi = w - 1 - lo
    cs = jnp.cumsum(x.astype(jnp.float32), axis=axis)
    pad = [(0, 0)] * x.ndim
    pad[axis] = (1, 0)
    cs = jnp.pad(cs, pad)
    idx = jnp.arange(n)
    top = jnp.minimum(idx + hi + 1, n)
    bot = jnp.maximum(idx - lo, 0)
    s = jnp.take(cs, top, axis=axis) - jnp.take(cs, bot, axis=axis)
    shape = [1] * x.ndim
    shape[axis] = n
    cnt = (top - bot).astype(jnp.float32).reshape(shape)
    return (s / cnt).astype(x.dtype)


def pool_mixer(u, pool_w, pool_scale, grid):
    B, N, _ = u.shape
    outs = []
    for gi, w in enumerate(POOL_WINDOWS):
        ug = u[..., gi * POOL_GROUP:(gi + 1) * POOL_GROUP]
        if grid:
            rows = N // GRID_W
            u2 = ug.reshape(B, rows, GRID_W, POOL_GROUP)
            m = box_mean(box_mean(u2, 1, w), 2, w).reshape(B, N, POOL_GROUP)
        else:
            m = box_mean(ug, 1, w)
        outs.append((m - ug) @ pool_w[gi])
    return jnp.concatenate(outs, axis=-1) * pool_scale


def conv_module(u, dw, dw_b, ln_g, ln_b, pw, pw_b):
    a, gt = jnp.split(u, 2, axis=-1)
    h = a * jax.nn.sigmoid(gt)
    h = lax.conv_general_dilated(h, dw[:, None, :], window_strides=(1,),
                                 padding=[(CONV_K // 2, CONV_K // 2)],
                                 dimension_numbers=('NWC', 'WIO', 'NWC'),
                                 feature_group_count=CONV_W) + dw_b
    h = jax.nn.silu(layer_norm(h, ln_g, ln_b))
    return h @ pw + pw_b


def gla_chunk_scan(q, k, v, logg, s0):
    B, N, H, _ = q.shape
    dv = v.shape[-1]
    nc = N // GLA_CHUNK

    def to_chunks(t):
        return t.astype(jnp.float32).reshape(B, nc, GLA_CHUNK, H, t.shape[-1]).transpose(1, 0, 3, 2, 4)

    lower = jnp.tril(jnp.ones((GLA_CHUNK, GLA_CHUNK), dtype=bool))[:, :, None]

    def step(s, inp):
        qc, kc, vc, gc = inp
        b = jnp.cumsum(gc, axis=2)
        o_inter = jnp.einsum('bhik,bhkv->bhiv', qc * jnp.exp(b), s)
        diff = b[:, :, :, None, :] - b[:, :, None, :, :]
        decay = jnp.exp(jnp.where(lower, diff, -jnp.inf))
        att = jnp.einsum('bhijk,bhjk->bhij', qc[:, :, :, None, :] * decay, kc)
        o_intra = jnp.einsum('bhij,bhjv->bhiv', att, vc)
        b_last = b[:, :, -1:, :]
        s_new = jnp.exp(b_last[:, :, 0, :, None]) * s + jnp.einsum(
            'bhjk,bhjv->bhkv', kc * jnp.exp(b_last - b), vc)
        return s_new, o_inter + o_intra

    s_fin, o = lax.scan(step, s0, (to_chunks(q), to_chunks(k), to_chunks(v), to_chunks(logg)))
    o = o.transpose(1, 0, 3, 2, 4).reshape(B, N, H, dv)
    return o, s_fin


def gla_bidir(q, k, v, g_f, g_b, s_f, s_b):
    o_f, s_f_fin = gla_chunk_scan(q, k, v, g_f, s_f)
    flip = lambda t: jnp.flip(t, axis=1)
    o_b, s_b_fin = gla_chunk_scan(flip(q), flip(k), flip(v), flip(g_b), s_b)
    return o_f + flip(o_b), s_f_fin, s_b_fin


def gla_inputs(parts, w_gk2, b_gk):
    q, k, v, _, lr_f, lr_b = parts[:6]
    B, N, _ = q.shape
    heads_k = lambda t: t.reshape(B, N, GLA_HEADS, GLA_HEAD_K)

    def log_decay(lr, d):
        logit = (lr @ w_gk2[d] + b_gk[d]).astype(jnp.float32)
        return heads_k(jax.nn.log_sigmoid(logit) / GATE_NORM)

    return (heads_k(q) * GLA_HEAD_K ** -0.5, heads_k(k),
            v.reshape(B, N, GLA_HEADS, GLA_HEAD_V), log_decay(lr_f, 0), log_decay(lr_b, 1))


def mixer_output(parts, o_gla, grid, gla_norm_g, pool_w, pool_scale, conv_dw, conv_dw_b,
                 conv_ln_g, conv_ln_b, conv_pw, conv_pw_b, w_out):
    g = parts[3]
    B, N = g.shape[:2]
    y_gla = rms_norm(o_gla, gla_norm_g).reshape(B, N, GLA_DV).astype(g.dtype) * jax.nn.silu(g)
    y_pool = pool_mixer(parts[6], pool_w, pool_scale, grid)
    y_conv = conv_module(parts[7], conv_dw, conv_dw_b, conv_ln_g, conv_ln_b, conv_pw, conv_pw_b)
    return jnp.concatenate([y_gla, y_pool, y_conv], axis=-1) @ w_out


def token_mix(h_lat, h_ctx, w_in, w_gk2, b_gk, gla_norm_g, pool_w, pool_scale, conv_dw, conv_dw_b,
              conv_ln_g, conv_ln_b, conv_pw, conv_pw_b, w_out, need_ctx):
    parts_ctx = jnp.split(h_ctx @ w_in, SPLIT_IDX, axis=-1)
    parts_lat = jnp.split(h_lat @ w_in, SPLIT_IDX, axis=-1)
    B = h_ctx.shape[0]
    zero = jnp.zeros((B, GLA_HEADS, GLA_HEAD_K, GLA_HEAD_V), jnp.float32)
    o_ctx, s_f, s_b = gla_bidir(*gla_inputs(parts_ctx, w_gk2, b_gk), zero, zero)
    o_lat, _, _ = gla_bidir(*gla_inputs(parts_lat, w_gk2, b_gk), s_f, s_b)
    tail = (gla_norm_g, pool_w, pool_scale, conv_dw, conv_dw_b, conv_ln_g, conv_ln_b, conv_pw, conv_pw_b, w_out)
    y_lat = mixer_output(parts_lat, o_lat, True, *tail)
    y_ctx = mixer_output(parts_ctx, o_ctx, False, *tail) if need_ctx else None
    return y_lat, y_ctx


def setup_inputs(seed: int = 0) -> dict:
    key = jax.random.key(seed)
    ks = jax.random.split(key, 26)
    D = D_MODEL
    nrm = lambda k, shape, scale: jax.random.normal(k, shape, jnp.float32) * scale
    return {
        "x": nrm(ks[0], (BATCH, SEQ, D), 1.0),
        "c": nrm(ks[1], (BATCH, D), 1.0),
        "ctx": nrm(ks[2], (BATCH, CTX_LEN, D), 1.0),
        "c_ctx": nrm(ks[3], (D,), 1.0),
        "w_ada": nrm(ks[4], (DEPTH, D, N_MOD * D), 0.5 * D ** -0.5),
        "b_ada": nrm(ks[5], (DEPTH, N_MOD * D), 0.02),
        "norm_g": 1.0 + nrm(ks[6], (DEPTH, 6, D), 0.05),
        "ffn1_up": nrm(ks[7], (DEPTH, D, 2 * D_FF), D ** -0.5),
        "ffn1_down": nrm(ks[8], (DEPTH, D_FF, D), D_FF ** -0.5),
        "ffn2_up": nrm(ks[9], (DEPTH, D, 2 * D_FF), D ** -0.5),
        "ffn2_down": nrm(ks[10], (DEPTH, D_FF, D), D_FF ** -0.5),
        "w_in": nrm(ks[11], (DEPTH, D, D_IN), D ** -0.5),
        "w_gk2": nrm(ks[12], (DEPTH, 2, GLA_LOWRANK, GLA_DK), GLA_LOWRANK ** -0.5),
        "b_gk": nrm(ks[13], (DEPTH, 2, GLA_DK), 0.1),
        "gla_norm_g": 1.0 + nrm(ks[14], (DEPTH, GLA_HEAD_V), 0.05),
        "pool_w": nrm(ks[15], (DEPTH, len(POOL_WINDOWS), POOL_GROUP, POOL_GROUP), POOL_GROUP ** -0.5),
        "pool_scale": 1.0 + nrm(ks[16], (DEPTH, POOL_W), 0.05),
        "conv_dw": nrm(ks[17], (DEPTH, CONV_K, CONV_W), CONV_K ** -0.5),
        "conv_dw_b": nrm(ks[18], (DEPTH, CONV_W), 0.02),
        "conv_ln_g": 1.0 + nrm(ks[19], (DEPTH, CONV_W), 0.05),
        "conv_ln_b": nrm(ks[20], (DEPTH, CONV_W), 0.02),
        "conv_pw": nrm(ks[21], (DEPTH, CONV_W, CONV_W), CONV_W ** -0.5),
        "conv_pw_b": nrm(ks[22], (DEPTH, CONV_W), 0.02),
        "w_out": nrm(ks[23], (DEPTH, MIX_W, D), MIX_W ** -0.5),
    }


def reference(x, c, ctx, c_ctx, w_ada, b_ada, norm_g, ffn1_up, ffn1_down, ffn2_up, ffn2_down,
              w_in, w_gk2, b_gk, gla_norm_g, pool_w, pool_scale, conv_dw, conv_dw_b,
              conv_ln_g, conv_ln_b, conv_pw, conv_pw_b, w_out):
    for l in range(DEPTH):
        last = l == DEPTH - 1
        ml = jnp.split((jax.nn.silu(c) @ w_ada[l] + b_ada[l])[:, None, :], N_MOD, axis=-1)
        mc = jnp.split((jax.nn.silu(c_ctx) @ w_ada[l] + b_ada[l])[None, None, :], N_MOD, axis=-1)
        gains = norm_g[l]
        x = sub_out(x, swiglu_ffn(sub_in(x, gains, ml, 0), ffn1_up[l], ffn1_down[l]), gains, ml, 0, 0.5)
        ctx = sub_out(ctx, swiglu_ffn(sub_in(ctx, gains, mc, 0), ffn1_up[l], ffn1_down[l]), gains, mc, 0, 0.5)
        y_lat, y_ctx = token_mix(sub_in(x, gains, ml, 1), sub_in(ctx, gains, mc, 1),
                                 w_in[l], w_gk2[l], b_gk[l], gla_norm_g[l], pool_w[l], pool_scale[l],
                                 conv_dw[l], conv_dw_b[l], conv_ln_g[l], conv_ln_b[l], conv_pw[l],
                                 conv_pw_b[l], w_out[l], not last)
        x = sub_out(x, y_lat, gains, ml, 1, 1.0)
        x = sub_out(x, swiglu_ffn(sub_in(x, gains, ml, 2), ffn2_up[l], ffn2_down[l]), gains, ml, 2, 0.5)
        if not last:
            ctx = sub_out(ctx, y_ctx, gains, mc, 1, 1.0)
            ctx = sub_out(ctx, swiglu_ffn(sub_in(ctx, gains, mc, 2), ffn2_up[l], ffn2_down[l]), gains, mc, 2, 0.5)
    return x
```

```python
import functools

import jax
import jax.numpy as jnp
from jax import lax
from jax.experimental import pallas as pl
from jax.experimental.pallas import tpu as pltpu

EPS = 1e-6
GRID_W = 64
N_MOD = 9
GLA_HEADS = 4
GLA_HEAD_K = 64
GLA_HEAD_V = 128
GLA_DK = GLA_HEADS * GLA_HEAD_K
GLA_DV = GLA_HEADS * GLA_HEAD_V
GLA_LOWRANK = 16
GATE_NORM = 16.0
CHUNK = 64
POOL_W = 256
POOL_HALF = (1, 2, 4, 8)
POOL_GROUP = 64
CONV_W = 256
CONV_K = 31
LR_PAD = 128
VMEM_LIMIT = 56 * 1024 * 1024

BF16 = jnp.bfloat16
F32 = jnp.float32


def _cparams(sem):
    return pltpu.CompilerParams(dimension_semantics=sem, vmem_limit_bytes=VMEM_LIMIT)


def _const_spec(shape):
    nd = len(shape)
    return pl.BlockSpec(shape, lambda *_: (0,) * nd, pipeline_mode=pl.Buffered(1))


def _rms(x, g):
    return x * lax.rsqrt(jnp.mean(x * x, axis=-1, keepdims=True) + EPS) * g


def _sub_in(x, mod_ref, gain_ref, i):
    shift = mod_ref[3 * i:3 * i + 1, :]
    scale = mod_ref[3 * i + 1:3 * i + 2, :]
    return _rms(x, gain_ref[2 * i:2 * i + 1, :]) * (1.0 + scale) + shift


def _sub_out(x, y, mod_ref, gain_ref, i, weight):
    gate = mod_ref[3 * i + 2:3 * i + 3, :]
    return x + weight * gate * _rms(y, gain_ref[2 * i + 1:2 * i + 2, :])


def _dot(a, b):
    return jnp.dot(a, b, preferred_element_type=F32)


def _mods_kernel(c_ref, w_ref, b_ref, o_ref):
    s = jax.nn.silu(c_ref[...]).astype(BF16)
    o_ref[...] = _dot(s, w_ref[...].astype(BF16)) + b_ref[...]


def _mods(cc, w_ada, b_ada):
    depth, d, nd = w_ada.shape
    rows = cc.shape[0]
    n_col = nd // d
    return pl.pallas_call(
        _mods_kernel,
        grid=(depth, n_col),
        in_specs=[pl.BlockSpec((rows, d), lambda l, j: (0, 0)),
                  pl.BlockSpec((None, d, d), lambda l, j: (l, 0, j)),
                  pl.BlockSpec((None, 1, d), lambda l, j: (l, 0, j))],
        out_specs=pl.BlockSpec((None, rows, d), lambda l, j: (l, 0, j)),
        out_shape=jax.ShapeDtypeStruct((depth, rows, nd), F32),
        compiler_params=_cparams(("arbitrary", "arbitrary")),
        name="mods",
    )(cc, w_ada, b_ada.reshape(depth, 1, nd))


def _ffn_kernel(x_ref, mod_ref, gain_ref, wup_ref, wdn_ref, o_ref, *, sub, weight, d_ff):
    x = x_ref[...]
    h = _sub_in(x, mod_ref, gain_ref, sub).astype(BF16)
    a = _dot(h, wup_ref[:, :d_ff])
    b = _dot(h, wup_ref[:, d_ff:])
    hh = (jax.nn.silu(a) * b).astype(BF16)
    y = _dot(hh, wdn_ref[...])
    o_ref[...] = _sub_out(x, y, mod_ref, gain_ref, sub, weight)


def _row_of_tile(tiles_per_batch, fixed_row):
    if fixed_row is not None:
        return lambda t: fixed_row
    return lambda t: t // tiles_per_batch


def _ffn(x, mods, gains, w_up, w_dn, *, layer, sub, weight, tile, tiles_per_batch, fixed_row):
    n, d = x.shape
    d_ff = w_dn.shape[0]
    row = _row_of_tile(tiles_per_batch, fixed_row)
    return pl.pallas_call(
        functools.partial(_ffn_kernel, sub=sub, weight=weight, d_ff=d_ff),
        grid=(n // tile,),
        in_specs=[pl.BlockSpec((tile, d), lambda t: (t, 0)),
                  pl.BlockSpec((None, None, N_MOD, d), lambda t: (layer, row(t), 0, 0)),
                  _const_spec(gains.shape),
                  _const_spec(w_up.shape),
                  _const_spec(w_dn.shape)],
        out_specs=pl.BlockSpec((tile, d), lambda t: (t, 0)),
        out_shape=jax.ShapeDtypeStruct((n, d), F32),
        compiler_params=_cparams(("parallel",)),
        name=f"ffn{sub}",
    )(x, mods, gains, w_up, w_dn)


def _log_sigmoid(x):
    return jnp.minimum(x, 0.0) - jnp.log(1.0 + jnp.exp(-jnp.abs(x)))


def _inproj_kernel(x_ref, mod_ref, gain_ref, w_ref, wgk_ref, bgk_ref,
                   q_ref, k_ref, v_ref, sg_ref, g_ref, pool_ref, conv_ref):
    h = _sub_in(x_ref[...], mod_ref, gain_ref, 1).astype(BF16)
    p = _dot(h, w_ref[...])
    c0 = 0
    q_ref[...] = (p[:, c0:c0 + GLA_DK] * (GLA_HEAD_K ** -0.5)).astype(BF16)
    c0 += GLA_DK
    k_ref[...] = p[:, c0:c0 + GLA_DK].astype(BF16)
    c0 += GLA_DK
    v_ref[...] = p[:, c0:c0 + GLA_DV].astype(BF16)
    c0 += GLA_DV
    sg_ref[...] = jax.nn.silu(p[:, c0:c0 + GLA_DV]).astype(BF16)
    c0 += GLA_DV
    pool_ref[...] = p[:, c0:c0 + POOL_W]
    c0 += POOL_W
    a = p[:, c0:c0 + CONV_W]
    gt = p[:, c0 + CONV_W:c0 + 2 * CONV_W]
    conv_ref[...] = a * jax.nn.sigmoid(gt)
    c0 += 2 * CONV_W
    lr = p[:, c0:c0 + LR_PAD].astype(BF16)
    logit = _dot(lr, wgk_ref[...]) + bgk_ref[...]
    logg = _log_sigmoid(logit) * (1.0 / GATE_NORM)
    g_ref[0] = logg[:, :GLA_DK]
    g_ref[1] = logg[:, GLA_DK:]


def _inproj(x, mods, gains, w_in_p, wgk_p, bgk_p, *, layer, tile, tiles_per_batch, fixed_row):
    n, d = x.shape
    row = _row_of_tile(tiles_per_batch, fixed_row)
    tok = lambda width: pl.BlockSpec((tile, width), lambda t: (t, 0))
    return pl.pallas_call(
        _inproj_kernel,
        grid=(n // tile,),
        in_specs=[tok(d),
                  pl.BlockSpec((None, None, N_MOD, d), lambda t: (layer, row(t), 0, 0)),
                  _const_spec(gains.shape),
                  _const_spec(w_in_p.shape),
                  _const_spec(wgk_p.shape),
                  _const_spec(bgk_p.shape)],
        out_specs=[tok(GLA_DK), tok(GLA_DK), tok(GLA_DV), tok(GLA_DV),
                   pl.BlockSpec((2, tile, GLA_DK), lambda t: (0, t, 0)),
                   tok(POOL_W), tok(CONV_W)],
        out_shape=[jax.ShapeDtypeStruct((n, GLA_DK), BF16),
                   jax.ShapeDtypeStruct((n, GLA_DK), BF16),
                   jax.ShapeDtypeStruct((n, GLA_DV), BF16),
                   jax.ShapeDtypeStruct((n, GLA_DV), BF16),
                   jax.ShapeDtypeStruct((2, n, GLA_DK), F32),
                   jax.ShapeDtypeStruct((n, POOL_W), F32),
                   jax.ShapeDtypeStruct((n, CONV_W), F32)],
        compiler_params=_cparams(("parallel",)),
        name="inproj",
    )(x, mods, gains, w_in_p, wgk_p, bgk_p)


_NT = (((1,), (1,)), ((), ()))
_TN = (((0,), (0,)), ((), ()))


def _gla_kernel(q_ref, k_ref, v_ref, g_ref, s0_ref, o_ref, s_ref, *, n_chunks):
    d = pl.program_id(1)
    bwd = d == 1

    @pl.when(pl.program_id(2) == 0)
    def _():
        s_ref[...] = s0_ref[...]

    row = lax.broadcasted_iota(jnp.int32, (CHUNK, CHUNK), 0)
    col = lax.broadcasted_iota(jnp.int32, (CHUNK, CHUNK), 1)
    tri = (row - col) * (1 - 2 * d) >= 0
    tri_b = jnp.where(tri, 1.0, 0.0).astype(BF16)
    tri4 = jnp.concatenate([tri] * GLA_HEADS, axis=0)
    lane_head = lax.broadcasted_iota(jnp.int32, (1, GLA_DK), 1) // GLA_HEAD_K
    head_mask = [jnp.where(lane_head == h, 1.0, 0.0).astype(F32) for h in range(GLA_HEADS)]

    def stack_heads(t):
        return jnp.concatenate([(t * m).astype(BF16) for m in head_mask], axis=0)

    for j in range(n_chunks):
        pos = j + d * (n_chunks - 1 - 2 * j)
        start = pl.multiple_of(pos * CHUNK, CHUNK)
        rows = pl.ds(start, CHUNK)
        qc = q_ref[rows, :].astype(F32)
        kc = k_ref[rows, :].astype(F32)
        vc = v_ref[rows, :]
        gc = g_ref[rows, :]
        g_hi = gc.astype(BF16)
        g_lo = (gc - g_hi.astype(F32)).astype(BF16)
        b = _dot(tri_b, g_hi) + _dot(tri_b, g_lo)
        b_mid = b[CHUNK // 2:CHUNK // 2 + 1, :]
        b_tot = jnp.where(bwd, b[0:1, :], b[CHUNK - 1:CHUNK, :])
        q_rel = stack_heads(qc * jnp.exp(b - b_mid))
        k_rel = (kc * jnp.exp(b_mid - b)).astype(BF16)
        q_in = stack_heads(qc * jnp.exp(b))
        k_end = stack_heads(kc * jnp.exp(b_tot - b))
        att = lax.dot_general(q_rel, k_rel, _NT, preferred_element_type=F32)
        att = jnp.where(tri4, att, 0.0).astype(BF16)
        s_prev = s_ref[...]
        o_inter = lax.dot_general(q_in, s_prev.astype(BF16), _NT, preferred_element_type=F32)
        outs = []
        for h in range(GLA_HEADS):
            hr = slice(h * CHUNK, (h + 1) * CHUNK)
            outs.append(_dot(att[hr, :], vc[:, h * GLA_HEAD_V:(h + 1) * GLA_HEAD_V]) + o_inter[hr, :])
        o_ref[rows, :] = jnp.concatenate(outs, axis=1)
        v_stack = jnp.concatenate([vc[:, h * GLA_HEAD_V:(h + 1) * GLA_HEAD_V] for h in range(GLA_HEADS)], axis=0)
        upd = lax.dot_general(v_stack, k_end, _TN, preferred_element_type=F32)
        s_ref[...] = s_prev * jnp.exp(b_tot) + upd


def _gla(q, k, v, g, s0, *, batch, tile):
    n = q.shape[0]
    nt = n // batch // tile
    tt = lambda b, d, t: b * nt + t + d * (nt - 1 - 2 * t)
    tok = lambda width: pl.BlockSpec((tile, width), lambda b, d, t: (tt(b, d, t), 0))
    state = pl.BlockSpec((None, None, GLA_HEAD_V, GLA_DK), lambda b, d, t: (b, d, 0, 0))
    return pl.pallas_call(
        functools.partial(_gla_kernel, n_chunks=tile // CHUNK),
        grid=(batch, 2, nt),
        in_specs=[tok(GLA_DK), tok(GLA_DK), tok(GLA_DV),
                  pl.BlockSpec((None, tile, GLA_DK), lambda b, d, t: (d, tt(b, d, t), 0)),
                  state],
        out_specs=[pl.BlockSpec((None, tile, GLA_DV), lambda b, d, t: (d, tt(b, d, t), 0)),
                   state],
        out_shape=[jax.ShapeDtypeStruct((2, n, GLA_DV), F32),
                   jax.ShapeDtypeStruct((batch, 2, GLA_HEAD_V, GLA_DK), F32)],
        compiler_params=_cparams(("parallel", "parallel", "arbitrary")),
        name="gla",
    )(q, k, v, g, s0)


def _window_sums(x, pos, period, halves, unit):
    n = x.shape[0]
    lane_hi = lax.broadcasted_iota(jnp.int32, (1, 128), 1) >= POOL_GROUP
    trail = {1: x}
    lead = {1: x}
    w = 1
    while w < max(halves):
        trail[2 * w] = trail[w] + jnp.where(pos >= w, pltpu.roll(trail[w], w * unit, axis=0), 0.0)
        lead[2 * w] = lead[w] + jnp.where(pos < period - w, pltpu.roll(lead[w], n - w * unit, axis=0), 0.0)
        w *= 2
    t_sel = jnp.where(lane_hi, trail[halves[1]], trail[halves[0]])
    l_sel = jnp.where(lane_hi, lead[halves[1]], lead[halves[0]])
    return jnp.where(pos >= 1, pltpu.roll(t_sel, unit, axis=0), 0.0) + l_sel


def _window_count(pos, period, half):
    return (jnp.minimum(pos + half, period) - jnp.maximum(pos - half, 0)).astype(F32)


def _poolconv_kernel(pool_ref, conv_ref, pw_ref, ps_ref, dw_ref, dwb_ref, lng_ref, lnb_ref, cpw_ref, cpb_ref,
                     o_ref, hpad_ref, *, grid2d):
    n = pool_ref.shape[0]
    x = pool_ref[...]
    tok = lax.broadcasted_iota(jnp.int32, (n, 1), 0)
    lane = lax.broadcasted_iota(jnp.int32, (1, POOL_W), 1)
    half = jnp.where(lane < 64, POOL_HALF[0], jnp.where(lane < 128, POOL_HALF[1],
                     jnp.where(lane < 192, POOL_HALF[2], POOL_HALF[3])))
    if grid2d:
        n_rows = n // GRID_W
        colp = tok % GRID_W
        rowp = tok // GRID_W
        s = jnp.concatenate([_window_sums(x[:, :128], rowp, n_rows, POOL_HALF[0:2], GRID_W),
                             _window_sums(x[:, 128:], rowp, n_rows, POOL_HALF[2:4], GRID_W)], axis=1)
        s = s / _window_count(rowp, n_rows, half)
        s = jnp.concatenate([_window_sums(s[:, :128], colp, GRID_W, POOL_HALF[0:2], 1),
                             _window_sums(s[:, 128:], colp, GRID_W, POOL_HALF[2:4], 1)], axis=1)
        m = s / _window_count(colp, GRID_W, half)
    else:
        s = jnp.concatenate([_window_sums(x[:, :128], tok, n, POOL_HALF[0:2], 1),
                             _window_sums(x[:, 128:], tok, n, POOL_HALF[2:4], 1)], axis=1)
        m = s / _window_count(tok, n, half)
    y_pool = _dot((m - x).astype(BF16), pw_ref[...]) * ps_ref[...]
    o_ref[:, :POOL_W] = y_pool.astype(o_ref.dtype)

    pad = 16
    hpad_ref[0:pad, :] = jnp.zeros((pad, CONV_W), F32)
    hpad_ref[pad + n:pad + n + pad, :] = jnp.zeros((pad, CONV_W), F32)
    hpad_ref[pad:pad + n, :] = conv_ref[...]
    big = hpad_ref[...]
    total = n + 2 * pad
    acc = jnp.zeros((n, CONV_W), F32)
    for sft in range(8):
        shifted = big if sft == 0 else pltpu.roll(big, total - sft, axis=0)
        for a in range(4):
            off = 8 * a + sft
            if off < 1 or off > CONV_K:
                continue
            acc = acc + shifted[8 * a:8 * a + n, :] * dw_ref[off - 1:off, :]
    hc = acc + dwb_ref[...]
    mu = jnp.mean(hc, axis=-1, keepdims=True)
    var = jnp.mean(jnp.square(hc - mu), axis=-1, keepdims=True)
    hn = (hc - mu) * lax.rsqrt(var + EPS) * lng_ref[...] + lnb_ref[...]
    y_conv = _dot(jax.nn.silu(hn).astype(BF16), cpw_ref[...]) + cpb_ref[...]
    o_ref[:, POOL_W:] = y_conv.astype(o_ref.dtype)


def _poolconv(pool_in, conv_in, weights, *, seq, grid2d):
    n = pool_in.shape[0]
    tok = lambda width: pl.BlockSpec((seq, width), lambda b: (b, 0))
    return pl.pallas_call(
        functools.partial(_poolconv_kernel, grid2d=grid2d),
        grid=(n // seq,),
        in_specs=[tok(POOL_W), tok(CONV_W)] + [_const_spec(w.shape) for w in weights],
        out_specs=tok(POOL_W + CONV_W),
        out_shape=jax.ShapeDtypeStruct((n, POOL_W + CONV_W), BF16),
        scratch_shapes=[pltpu.VMEM((seq + 32, CONV_W), F32)],
        compiler_params=_cparams(("parallel",)),
        name="poolconv",
    )(pool_in, conv_in, *weights)


def _mixout_kernel(x_ref, o_ref, sg_ref, pc_ref, mod_ref, gain_ref, gng_ref, wout_ref, out_ref):
    o = o_ref[0] + o_ref[1]
    sg = sg_ref[...].astype(F32)
    ys = []
    for h in range(GLA_HEADS):
        hs = slice(h * GLA_HEAD_V, (h + 1) * GLA_HEAD_V)
        ys.append((_rms(o[:, hs], gng_ref[...]) * sg[:, hs]).astype(BF16))
    y_gla = jnp.concatenate(ys, axis=1)
    y = _dot(y_gla, wout_ref[:GLA_DV, :]) + _dot(pc_ref[...], wout_ref[GLA_DV:, :])
    out_ref[...] = _sub_out(x_ref[...], y, mod_ref, gain_ref, 1, 1.0)


def _mixout(x, o, sg, pc, mods, gains, gng, w_out, *, layer, tile, tiles_per_batch, fixed_row):
    n, d = x.shape
    row = _row_of_tile(tiles_per_batch, fixed_row)
    tok = lambda width: pl.BlockSpec((tile, width), lambda t: (t, 0))
    return pl.pallas_call(
        _mixout_kernel,
        grid=(n // tile,),
        in_specs=[tok(d),
                  pl.BlockSpec((2, tile, GLA_DV), lambda t: (0, t, 0)),
                  tok(GLA_DV), tok(POOL_W + CONV_W),
                  pl.BlockSpec((None, None, N_MOD, d), lambda t: (layer, row(t), 0, 0)),
                  _const_spec(gains.shape),
                  _const_spec(gng.shape),
                  _const_spec(w_out.shape)],
        out_specs=tok(d),
        out_shape=jax.ShapeDtypeStruct((n, d), F32),
        compiler_params=_cparams(("parallel",)),
        name="mixout",
    )(x, o, sg, pc, mods, gains, gng, w_out)


def _block_diag(blocks):
    g, a, b = blocks.shape
    out = jnp.zeros((g * a, g * b), blocks.dtype)
    for i in range(g):
        out = out.at[i * a:(i + 1) * a, i * b:(i + 1) * b].set(blocks[i])
    return out


def _pick_tile(n, want):
    t = min(n, want)
    assert n % t == 0
    return t


def kernel(x, c, ctx, c_ctx, w_ada, b_ada, norm_g, ffn1_up, ffn1_down, ffn2_up, ffn2_down, w_in, w_gk2, b_gk,
           gla_norm_g, pool_w, pool_scale, conv_dw, conv_dw_b, conv_ln_g, conv_ln_b, conv_pw, conv_pw_b, w_out):
    batch, seq, d = x.shape
    ctx_len = ctx.shape[1]
    depth = w_ada.shape[0]
    assert seq % GRID_W == 0 and seq % CHUNK == 0 and ctx_len % CHUNK == 0

    xf = x.reshape(batch * seq, d)
    cf = ctx.reshape(batch * ctx_len, d)

    ctx_row = batch
    n_rows = -(-(batch + 1) // 8) * 8
    cc = jnp.zeros((n_rows, d), F32).at[:batch].set(c).at[ctx_row].set(c_ctx)
    mods = _mods(cc, w_ada, b_ada).reshape(depth, n_rows, N_MOD, d)

    lat_tile = _pick_tile(seq, 256)
    ctx_tile = _pick_tile(batch * ctx_len, 256)
    lat = dict(tile=lat_tile, tiles_per_batch=seq // lat_tile, fixed_row=None)
    cx = dict(tile=ctx_tile, tiles_per_batch=None, fixed_row=ctx_row)
    gla_tile = _pick_tile(seq, 512)

    for l in range(depth):
        last = l == depth - 1
        gains = norm_g[l]
        up1, dn1 = ffn1_up[l].astype(BF16), ffn1_down[l].astype(BF16)
        up2, dn2 = ffn2_up[l].astype(BF16), ffn2_down[l].astype(BF16)
        wi = w_in[l]
        o_lr = 2 * GLA_DK + 2 * GLA_DV
        o_pool = o_lr + 2 * GLA_LOWRANK
        w_in_p = jnp.concatenate(
            [wi[:, :o_lr], wi[:, o_pool:], wi[:, o_lr:o_pool],
             jnp.zeros((d, LR_PAD - 2 * GLA_LOWRANK), F32)], axis=1).astype(BF16)
        wgk_p = jnp.zeros((LR_PAD, 2 * GLA_DK), F32)
        wgk_p = wgk_p.at[:GLA_LOWRANK, :GLA_DK].set(w_gk2[l, 0])
        wgk_p = wgk_p.at[GLA_LOWRANK:2 * GLA_LOWRANK, GLA_DK:].set(w_gk2[l, 1]).astype(BF16)
        bgk_p = b_gk[l].reshape(1, 2 * GLA_DK)
        pc_weights = (_block_diag(pool_w[l]).astype(BF16), pool_scale[l].reshape(1, POOL_W),
                      jnp.concatenate([conv_dw[l], jnp.zeros((1, CONV_W), F32)], axis=0),
                      conv_dw_b[l].reshape(1, CONV_W), conv_ln_g[l].reshape(1, CONV_W),
                      conv_ln_b[l].reshape(1, CONV_W), conv_pw[l].astype(BF16), conv_pw_b[l].reshape(1, CONV_W))
        gng = gla_norm_g[l].reshape(1, GLA_HEAD_V)
        wo = w_out[l].astype(BF16)

        xf = _ffn(xf, mods, gains, up1, dn1, layer=l, sub=0, weight=0.5, **lat)
        cf = _ffn(cf, mods, gains, up1, dn1, layer=l, sub=0, weight=0.5, **cx)
        q_c, k_c, v_c, sg_c, g_c, pool_c, conv_c = _inproj(cf, mods, gains, w_in_p, wgk_p, bgk_p, layer=l, **cx)
        q_l, k_l, v_l, sg_l, g_l, pool_l, conv_l = _inproj(xf, mods, gains, w_in_p, wgk_p, bgk_p, layer=l, **lat)
        zero = jnp.zeros((batch, 2, GLA_HEAD_V, GLA_DK), F32)
        o_c, s_c = _gla(q_c, k_c, v_c, g_c, zero, batch=batch, tile=ctx_len)
        o_l, _ = _gla(q_l, k_l, v_l, g_l, s_c, batch=batch, tile=gla_tile)
        pc_l = _poolconv(pool_l, conv_l, pc_weights, seq=seq, grid2d=True)
        xf = _mixout(xf, o_l, sg_l, pc_l, mods, gains, gng, wo, layer=l, **lat)
        xf = _ffn(xf, mods, gains, up2, dn2, layer=l, sub=2, weight=0.5, **lat)
        if not last:
            pc_c = _poolconv(pool_c, conv_c, pc_weights, seq=ctx_len, grid2d=False)
            cf = _mixout(cf, o_c, sg_c, pc_c, mods, gains, gng, wo, layer=l, **cx)
            cf = _ffn(cf, mods, gains, up2, dn2, layer=l, sub=2, weight=0.5, **cx)
    return xf.reshape(batch, seq, d)
```

```python
import functools
import math

import numpy as np
import jax
import jax.numpy as jnp
from jax import lax
from jax.experimental import pallas as pl
from jax.experimental.pallas import tpu as pltpu

EPS = 1e-6
GRID_W = 64
N_MOD = 9
GLA_HEADS = 4
GLA_HEAD_K = 64
GLA_HEAD_V = 128
GLA_DK = GLA_HEADS * GLA_HEAD_K
GLA_DV = GLA_HEADS * GLA_HEAD_V
GLA_LOWRANK = 16
GATE_NORM = 16.0
CHUNK = 64
LANES = 128
ROWS = 256
POOL_W = 256
POOL_HALF = (1, 2, 4, 8)
POOL_GROUP = 64
CONV_W = 256
CONV_K = 31
LR_PAD = 128
VMEM_LIMIT = 56 * 1024 * 1024

BF16 = jnp.bfloat16
F32 = jnp.float32


def _cparams(sem):
    return pltpu.CompilerParams(dimension_semantics=sem, vmem_limit_bytes=VMEM_LIMIT)


def _const_spec(shape):
    nd = len(shape)
    return pl.BlockSpec(shape, lambda *_: (0,) * nd, pipeline_mode=pl.Buffered(1))


def _rms(x, g):
    return x * lax.rsqrt(jnp.mean(x * x, axis=-1, keepdims=True) + EPS) * g


def _sub_in(x, mod_ref, gain_ref, i):
    shift = mod_ref[3 * i:3 * i + 1, :]
    scale = mod_ref[3 * i + 1:3 * i + 2, :]
    return _rms(x, gain_ref[2 * i:2 * i + 1, :]) * (1.0 + scale) + shift


def _sub_out(x, y, mod_ref, gain_ref, i, weight):
    gate = mod_ref[3 * i + 2:3 * i + 3, :]
    return x + weight * gate * _rms(y, gain_ref[2 * i + 1:2 * i + 2, :])


def _dot(a, b):
    return jnp.dot(a, b, preferred_element_type=F32)


def _row_of_tile(tiles_per_batch, fixed_row):
    if fixed_row is not None:
        return lambda t: fixed_row
    return lambda t: t // tiles_per_batch


def _row_groups(n):
    return [slice(r, r + ROWS) for r in range(0, n, ROWS)]


def _mods_kernel(c_ref, w_ref, b_ref, o_ref):
    s = jax.nn.silu(c_ref[...]).astype(BF16)
    o_ref[...] = _dot(s, w_ref[...].astype(BF16)) + b_ref[...]


def _mods(cc, w_ada, b_ada):
    depth, d, nd = w_ada.shape
    rows = cc.shape[0]
    n_col = nd // d
    return pl.pallas_call(
        _mods_kernel,
        grid=(depth, n_col),
        in_specs=[pl.BlockSpec((rows, d), lambda l, j: (0, 0)),
                  pl.BlockSpec((None, d, d), lambda l, j: (l, 0, j)),
                  pl.BlockSpec((None, 1, d), lambda l, j: (l, 0, j))],
        out_specs=pl.BlockSpec((None, rows, d), lambda l, j: (l, 0, j)),
        out_shape=jax.ShapeDtypeStruct((depth, rows, nd), F32),
        compiler_params=_cparams(("arbitrary", "arbitrary")),
        name="mods",
    )(cc, w_ada, b_ada.reshape(depth, 1, nd))


def _ffn_rows(x, mod_ref, gain_ref, wup_ref, wdn_ref, sub, weight):
    d_ff = wdn_ref.shape[0]
    h = _sub_in(x, mod_ref, gain_ref, sub).astype(BF16)
    a = _dot(h, wup_ref[:, :d_ff])
    b = _dot(h, wup_ref[:, d_ff:])
    hh = (jax.nn.silu(a) * b).astype(BF16)
    y = _dot(hh, wdn_ref[...])
    return _sub_out(x, y, mod_ref, gain_ref, sub, weight)


def _ffn_kernel(x_ref, mod_ref, gain_ref, wup_ref, wdn_ref, o_ref, *, sub, weight):
    for rows in _row_groups(x_ref.shape[0]):
        o_ref[rows, :] = _ffn_rows(x_ref[rows, :], mod_ref, gain_ref, wup_ref, wdn_ref, sub, weight)


def _ffn(x, mods, gains, w_up, w_dn, *, layer, sub, weight, tile, tiles_per_batch, fixed_row):
    n, d = x.shape
    row = _row_of_tile(tiles_per_batch, fixed_row)
    return pl.pallas_call(
        functools.partial(_ffn_kernel, sub=sub, weight=weight),
        grid=(n // tile,),
        in_specs=[pl.BlockSpec((tile, d), lambda t: (t, 0)),
                  pl.BlockSpec((None, None, N_MOD, d), lambda t: (layer, row(t), 0, 0)),
                  _const_spec(gains.shape),
                  _const_spec(w_up.shape),
                  _const_spec(w_dn.shape)],
        out_specs=pl.BlockSpec((tile, d), lambda t: (t, 0)),
        out_shape=jax.ShapeDtypeStruct((n, d), F32),
        compiler_params=_cparams(("parallel",)),
        name=f"ffn{sub}",
    )(x, mods, gains, w_up, w_dn)


def _log_sigmoid(x):
    return jnp.minimum(x, 0.0) - jnp.log(1.0 + jnp.exp(-jnp.abs(x)))


def _inproj_kernel(x_ref, mod_ref, gain_ref, w_ref, wgk_ref, bgk_ref, tri_ref,
                   q_ref, k_ref, v_ref, sg_ref, b_ref, pool_ref, conv_ref):
    for rows in _row_groups(x_ref.shape[0]):
        h = _sub_in(x_ref[rows, :], mod_ref, gain_ref, 1).astype(BF16)
        p = _dot(h, w_ref[...])
        c0 = 0
        q_ref[rows, :] = (p[:, c0:c0 + GLA_DK] * (GLA_HEAD_K ** -0.5)).astype(BF16)
        c0 += GLA_DK
        k_ref[rows, :] = p[:, c0:c0 + GLA_DK].astype(BF16)
        c0 += GLA_DK
        v_ref[rows, :] = p[:, c0:c0 + GLA_DV].astype(BF16)
        c0 += GLA_DV
        sg_ref[rows, :] = jax.nn.silu(p[:, c0:c0 + GLA_DV]).astype(BF16)
        c0 += GLA_DV
        pool_ref[rows, :] = p[:, c0:c0 + POOL_W]
        c0 += POOL_W
        a = p[:, c0:c0 + CONV_W]
        gt = p[:, c0 + CONV_W:c0 + 2 * CONV_W]
        conv_ref[rows, :] = a * jax.nn.sigmoid(gt)
        c0 += 2 * CONV_W
        lr = p[:, c0:c0 + LR_PAD].astype(BF16)
        logit = _dot(lr, wgk_ref[...]) + bgk_ref[...]
        lg = _log_sigmoid(logit) * (math.log2(math.e) / GATE_NORM)
        hi = lg.astype(BF16)
        lo = (lg - hi.astype(F32)).astype(BF16)
        for d in range(2):
            cols = slice(d * GLA_DK, (d + 1) * GLA_DK)
            b_ref[d, rows, :] = _dot(tri_ref[d], hi[:, cols]) + _dot(tri_ref[d], lo[:, cols])


def _chunk_tri():
    i = np.arange(ROWS)[:, None]
    j = np.arange(ROWS)[None, :]
    same = (i // CHUNK) == (j // CHUNK)
    return jnp.asarray(np.stack([same & (j <= i), same & (j >= i)]).astype(np.float32), BF16)


def _inproj(x, mods, gains, w_in_p, wgk_p, bgk_p, *, layer, tile, tiles_per_batch, fixed_row):
    n, d = x.shape
    row = _row_of_tile(tiles_per_batch, fixed_row)
    tok = lambda width: pl.BlockSpec((tile, width), lambda t: (t, 0))
    tri = _chunk_tri()
    return pl.pallas_call(
        _inproj_kernel,
        grid=(n // tile,),
        in_specs=[tok(d),
                  pl.BlockSpec((None, None, N_MOD, d), lambda t: (layer, row(t), 0, 0)),
                  _const_spec(gains.shape),
                  _const_spec(w_in_p.shape),
                  _const_spec(wgk_p.shape),
                  _const_spec(bgk_p.shape),
                  _const_spec(tri.shape)],
        out_specs=[tok(GLA_DK), tok(GLA_DK), tok(GLA_DV), tok(GLA_DV),
                   pl.BlockSpec((2, tile, GLA_DK), lambda t: (0, t, 0)),
                   tok(POOL_W), tok(CONV_W)],
        out_shape=[jax.ShapeDtypeStruct((n, GLA_DK), BF16),
                   jax.ShapeDtypeStruct((n, GLA_DK), BF16),
                   jax.ShapeDtypeStruct((n, GLA_DV), BF16),
                   jax.ShapeDtypeStruct((n, GLA_DV), BF16),
                   jax.ShapeDtypeStruct((2, n, GLA_DK), F32),
                   jax.ShapeDtypeStruct((n, POOL_W), F32),
                   jax.ShapeDtypeStruct((n, CONV_W), F32)],
        compiler_params=_cparams(("parallel",)),
        name="inproj",
    )(x, mods, gains, w_in_p, wgk_p, bgk_p, tri)


_NT = (((1,), (1,)), ((), ()))
_TN = (((0,), (0,)), ((), ()))


def _gla_direction(q_ref, k_ref, v_ref, b_ref, o_ref, state, *, bwd):
    n_chunks = q_ref.shape[0] // CHUNK
    lane_lo = lax.broadcasted_iota(jnp.int32, (1, LANES), 1) < GLA_HEAD_K
    row = lax.broadcasted_iota(jnp.int32, (CHUNK, CHUNK), 0)
    col = lax.broadcasted_iota(jnp.int32, (CHUNK, CHUNK), 1)
    tri = (col >= row) if bwd else (col <= row)
    tri2 = jnp.concatenate([tri, tri], axis=0)

    def stack(t):
        return jnp.concatenate([jnp.where(lane_lo, t, 0.0), jnp.where(lane_lo, 0.0, t)], axis=0).astype(BF16)

    local = []
    for c in range(n_chunks):
        rows = slice(c * CHUNK, (c + 1) * CHUNK)
        qc = q_ref[rows, :].astype(F32)
        kc = k_ref[rows, :].astype(F32)
        vc = v_ref[rows, :]
        bc = b_ref[rows, :]
        b_mid = bc[CHUNK // 2:CHUNK // 2 + 1, :]
        b_tot = bc[0:1, :] if bwd else bc[CHUNK - 1:CHUNK, :]
        q_rel = qc * jnp.exp2(bc - b_mid)
        k_rel = (kc * jnp.exp2(b_mid - bc)).astype(BF16)
        q_in = qc * jnp.exp2(bc)
        k_end = kc * jnp.exp2(b_tot - bc)
        decay = jnp.exp2(b_tot)
        per_tile = []
        for t in range(GLA_DK // LANES):
            ls = slice(t * LANES, (t + 1) * LANES)
            att = lax.dot_general(stack(q_rel[:, ls]), k_rel[:, ls], _NT, preferred_element_type=F32)
            att = jnp.where(tri2, att, 0.0).astype(BF16)
            v_pair = [vc[:, (2 * t + hh) * GLA_HEAD_V:(2 * t + hh + 1) * GLA_HEAD_V] for hh in range(2)]
            o_intra = [_dot(att[hh * CHUNK:(hh + 1) * CHUNK, :], v_pair[hh]) for hh in range(2)]
            upd = lax.dot_general(jnp.concatenate(v_pair, axis=0), stack(k_end[:, ls]), _TN,
                                  preferred_element_type=F32)
            per_tile.append((stack(q_in[:, ls]), o_intra, upd, decay[:, ls]))
        local.append(per_tile)

    for c in (reversed(range(n_chunks)) if bwd else range(n_chunks)):
        outs = []
        for t, (q_in_st, o_intra, upd, decay) in enumerate(local[c]):
            o_inter = lax.dot_general(q_in_st, state[t].astype(BF16), _NT, preferred_element_type=F32)
            outs += [o_intra[hh] + o_inter[hh * CHUNK:(hh + 1) * CHUNK, :] for hh in range(2)]
            state[t] = state[t] * decay + upd
        o_ref[c * CHUNK:(c + 1) * CHUNK, :] = jnp.concatenate(outs, axis=1).astype(o_ref.dtype)
    return state


def _gla_kernel(qf_ref, kf_ref, vf_ref, bf_ref, qb_ref, kb_ref, vb_ref, bb_ref, s0_ref,
                of_ref, ob_ref, s_ref):
    @pl.when(pl.program_id(1) == 0)
    def _():
        s_ref[...] = s0_ref[...]

    n_lane_tiles = GLA_DK // LANES
    dirs = ((qf_ref, kf_ref, vf_ref, bf_ref, of_ref), (qb_ref, kb_ref, vb_ref, bb_ref, ob_ref))
    for d, refs in enumerate(dirs):
        state = [s_ref[d, :, t * LANES:(t + 1) * LANES] for t in range(n_lane_tiles)]
        state = _gla_direction(*refs, state, bwd=(d == 1))
        for t in range(n_lane_tiles):
            s_ref[d, :, t * LANES:(t + 1) * LANES] = state[t]


def _gla(q, k, v, b, s0, *, batch, tile):
    n = q.shape[0]
    nt = n // batch // tile
    fwd = lambda i, t: i * nt + t
    bwd = lambda i, t: i * nt + nt - 1 - t
    tok = lambda width, at: pl.BlockSpec((tile, width), lambda i, t: (at(i, t), 0))
    dec = lambda d, at: pl.BlockSpec((None, tile, GLA_DK), lambda i, t: (d, at(i, t), 0))
    state = pl.BlockSpec((None, 2, GLA_HEAD_V, GLA_DK), lambda i, t: (i, 0, 0, 0))
    return pl.pallas_call(
        _gla_kernel,
        grid=(batch, nt),
        in_specs=[tok(GLA_DK, fwd), tok(GLA_DK, fwd), tok(GLA_DV, fwd), dec(0, fwd),
                  tok(GLA_DK, bwd), tok(GLA_DK, bwd), tok(GLA_DV, bwd), dec(1, bwd),
                  state],
        out_specs=[tok(GLA_DV, fwd), tok(GLA_DV, bwd), state],
        out_shape=[jax.ShapeDtypeStruct((n, GLA_DV), BF16),
                   jax.ShapeDtypeStruct((n, GLA_DV), BF16),
                   jax.ShapeDtypeStruct((batch, 2, GLA_HEAD_V, GLA_DK), F32)],
        compiler_params=_cparams(("parallel", "arbitrary")),
        name="gla",
    )(q, k, v, b, q, k, v, b, s0)


def _window_sums(x, pos, period, halves, unit):
    n = x.shape[0]
    lane_hi = lax.broadcasted_iota(jnp.int32, (1, LANES), 1) >= POOL_GROUP
    trail = {1: x}
    lead = {1: x}
    w = 1
    while w < max(halves):
        trail[2 * w] = trail[w] + jnp.where(pos >= w, pltpu.roll(trail[w], w * unit, axis=0), 0.0)
        lead[2 * w] = lead[w] + jnp.where(pos < period - w, pltpu.roll(lead[w], n - w * unit, axis=0), 0.0)
        w *= 2
    t_sel = jnp.where(lane_hi, trail[halves[1]], trail[halves[0]])
    l_sel = jnp.where(lane_hi, lead[halves[1]], lead[halves[0]])
    return jnp.where(pos >= 1, pltpu.roll(t_sel, unit, axis=0), 0.0) + l_sel


def _window_count(pos, period, half):
    return (jnp.minimum(pos + half, period) - jnp.maximum(pos - half, 0)).astype(F32)


def _poolconv_kernel(pool_ref, conv_ref, pw_ref, ps_ref, dw_ref, dwb_ref, lng_ref, lnb_ref, cpw_ref, cpb_ref,
                     o_ref, hpad_ref, *, grid2d):
    n = pool_ref.shape[0]
    x = pool_ref[...]
    tok = lax.broadcasted_iota(jnp.int32, (n, 1), 0)
    lane = lax.broadcasted_iota(jnp.int32, (1, POOL_W), 1)
    half = jnp.where(lane < 64, POOL_HALF[0], jnp.where(lane < 128, POOL_HALF[1],
                     jnp.where(lane < 192, POOL_HALF[2], POOL_HALF[3])))
    if grid2d:
        n_rows = n // GRID_W
        colp = tok % GRID_W
        rowp = tok // GRID_W
        s = jnp.concatenate([_window_sums(x[:, :128], rowp, n_rows, POOL_HALF[0:2], GRID_W),
                             _window_sums(x[:, 128:], rowp, n_rows, POOL_HALF[2:4], GRID_W)], axis=1)
        s = s / _window_count(rowp, n_rows, half)
        s = jnp.concatenate([_window_sums(s[:, :128], colp, GRID_W, POOL_HALF[0:2], 1),
                             _window_sums(s[:, 128:], colp, GRID_W, POOL_HALF[2:4], 1)], axis=1)
        m = s / _window_count(colp, GRID_W, half)
    else:
        s = jnp.concatenate([_window_sums(x[:, :128], tok, n, POOL_HALF[0:2], 1),
                             _window_sums(x[:, 128:], tok, n, POOL_HALF[2:4], 1)], axis=1)
        m = s / _window_count(tok, n, half)
    y_pool = _dot((m - x).astype(BF16), pw_ref[...]) * ps_ref[...]
    o_ref[:, :POOL_W] = y_pool.astype(o_ref.dtype)

    pad = 16
    hpad_ref[0:pad, :] = jnp.zeros((pad, CONV_W), F32)
    hpad_ref[pad + n:pad + n + pad, :] = jnp.zeros((pad, CONV_W), F32)
    hpad_ref[pad:pad + n, :] = conv_ref[...]
    big = hpad_ref[...]
    total = n + 2 * pad
    acc = jnp.zeros((n, CONV_W), F32)
    for sft in range(8):
        shifted = big if sft == 0 else pltpu.roll(big, total - sft, axis=0)
        for a in range(4):
            off = 8 * a + sft
            if off < 1 or off > CONV_K:
                continue
            acc = acc + shifted[8 * a:8 * a + n, :] * dw_ref[off - 1:off, :]
    hc = acc + dwb_ref[...]
    mu = jnp.mean(hc, axis=-1, keepdims=True)
    var = jnp.mean(jnp.square(hc - mu), axis=-1, keepdims=True)
    hn = (hc - mu) * lax.rsqrt(var + EPS) * lng_ref[...] + lnb_ref[...]
    y_conv = _dot(jax.nn.silu(hn).astype(BF16), cpw_ref[...]) + cpb_ref[...]
    o_ref[:, POOL_W:] = y_conv.astype(o_ref.dtype)


def _poolconv(pool_in, conv_in, weights, *, seq, grid2d):
    n = pool_in.shape[0]
    tok = lambda width: pl.BlockSpec((seq, width), lambda b: (b, 0))
    return pl.pallas_call(
        functools.partial(_poolconv_kernel, grid2d=grid2d),
        grid=(n // seq,),
        in_specs=[tok(POOL_W), tok(CONV_W)] + [_const_spec(w.shape) for w in weights],
        out_specs=tok(POOL_W + CONV_W),
        out_shape=jax.ShapeDtypeStruct((n, POOL_W + CONV_W), BF16),
        scratch_shapes=[pltpu.VMEM((seq + 32, CONV_W), F32)],
        compiler_params=_cparams(("parallel",)),
        name="poolconv",
    )(pool_in, conv_in, *weights)


def _mixffn_kernel(x_ref, of_ref, ob_ref, sg_ref, pc_ref, mod_ref, gain_ref, gng_ref, wout_ref, wup_ref, wdn_ref,
                   out_ref):
    for rows in _row_groups(x_ref.shape[0]):
        o = of_ref[rows, :].astype(F32) + ob_ref[rows, :].astype(F32)
        sg = sg_ref[rows, :].astype(F32)
        ys = []
        for h in range(GLA_HEADS):
            hs = slice(h * GLA_HEAD_V, (h + 1) * GLA_HEAD_V)
            ys.append((_rms(o[:, hs], gng_ref[...]) * sg[:, hs]).astype(BF16))
        y_gla = jnp.concatenate(ys, axis=1)
        y = _dot(y_gla, wout_ref[:GLA_DV, :]) + _dot(pc_ref[rows, :], wout_ref[GLA_DV:, :])
        x = _sub_out(x_ref[rows, :], y, mod_ref, gain_ref, 1, 1.0)
        out_ref[rows, :] = _ffn_rows(x, mod_ref, gain_ref, wup_ref, wdn_ref, 2, 0.5)


def _mixffn(x, o_f, o_b, sg, pc, mods, gains, gng, w_out, w_up, w_dn, *, layer, tile, tiles_per_batch, fixed_row):
    n, d = x.shape
    row = _row_of_tile(tiles_per_batch, fixed_row)
    tok = lambda width: pl.BlockSpec((tile, width), lambda t: (t, 0))
    return pl.pallas_call(
        _mixffn_kernel,
        grid=(n // tile,),
        in_specs=[tok(d), tok(GLA_DV), tok(GLA_DV), tok(GLA_DV), tok(POOL_W + CONV_W),
                  pl.BlockSpec((None, None, N_MOD, d), lambda t: (layer, row(t), 0, 0)),
                  _const_spec(gains.shape),
                  _const_spec(gng.shape),
                  _const_spec(w_out.shape),
                  _const_spec(w_up.shape),
                  _const_spec(w_dn.shape)],
        out_specs=tok(d),
        out_shape=jax.ShapeDtypeStruct((n, d), F32),
        compiler_params=_cparams(("parallel",)),
        name="mixffn",
    )(x, o_f, o_b, sg, pc, mods, gains, gng, w_out, w_up, w_dn)


def _block_diag(blocks):
    g, a, b = blocks.shape
    out = jnp.zeros((g * a, g * b), blocks.dtype)
    for i in range(g):
        out = out.at[i * a:(i + 1) * a, i * b:(i + 1) * b].set(blocks[i])
    return out


def _pick_tile(n, want):
    t = min(n, want)
    assert n % t == 0 and t % ROWS == 0
    return t


def kernel(x, c, ctx, c_ctx, w_ada, b_ada, norm_g, ffn1_up, ffn1_down, ffn2_up, ffn2_down, w_in, w_gk2, b_gk,
           gla_norm_g, pool_w, pool_scale, conv_dw, conv_dw_b, conv_ln_g, conv_ln_b, conv_pw, conv_pw_b, w_out):
    batch, seq, d = x.shape
    ctx_len = ctx.shape[1]
    depth = w_ada.shape[0]
    assert seq % GRID_W == 0 and seq % CHUNK == 0 and ctx_len % CHUNK == 0

    xf = x.reshape(batch * seq, d)
    cf = ctx.reshape(batch * ctx_len, d)

    ctx_row = batch
    n_rows = -(-(batch + 1) // 8) * 8
    cc = jnp.zeros((n_rows, d), F32).at[:batch].set(c).at[ctx_row].set(c_ctx)
    mods = _mods(cc, w_ada, b_ada).reshape(depth, n_rows, N_MOD, d)

    lat_tile = _pick_tile(seq, 512)
    ctx_tile = _pick_tile(batch * ctx_len, 512)
    lat = dict(tile=lat_tile, tiles_per_batch=seq // lat_tile, fixed_row=None)
    cx = dict(tile=ctx_tile, tiles_per_batch=None, fixed_row=ctx_row)
    gla_tile = _pick_tile(seq, 512)

    for l in range(depth):
        last = l == depth - 1
        gains = norm_g[l]
        up1, dn1 = ffn1_up[l].astype(BF16), ffn1_down[l].astype(BF16)
        up2, dn2 = ffn2_up[l].astype(BF16), ffn2_down[l].astype(BF16)
        wi = w_in[l]
        o_lr = 2 * GLA_DK + 2 * GLA_DV
        o_pool = o_lr + 2 * GLA_LOWRANK
        w_in_p = jnp.concatenate(
            [wi[:, :o_lr], wi[:, o_pool:], wi[:, o_lr:o_pool],
             jnp.zeros((d, LR_PAD - 2 * GLA_LOWRANK), F32)], axis=1).astype(BF16)
        wgk_p = jnp.zeros((LR_PAD, 2 * GLA_DK), F32)
        wgk_p = wgk_p.at[:GLA_LOWRANK, :GLA_DK].set(w_gk2[l, 0])
        wgk_p = wgk_p.at[GLA_LOWRANK:2 * GLA_LOWRANK, GLA_DK:].set(w_gk2[l, 1]).astype(BF16)
        bgk_p = b_gk[l].reshape(1, 2 * GLA_DK)
        pc_weights = (_block_diag(pool_w[l]).astype(BF16), pool_scale[l].reshape(1, POOL_W),
                      jnp.concatenate([conv_dw[l], jnp.zeros((1, CONV_W), F32)], axis=0),
                      conv_dw_b[l].reshape(1, CONV_W), conv_ln_g[l].reshape(1, CONV_W),
                      conv_ln_b[l].reshape(1, CONV_W), conv_pw[l].astype(BF16), conv_pw_b[l].reshape(1, CONV_W))
        gng = gla_norm_g[l].reshape(1, GLA_HEAD_V)
        wo = w_out[l].astype(BF16)

        xf = _ffn(xf, mods, gains, up1, dn1, layer=l, sub=0, weight=0.5, **lat)
        cf = _ffn(cf, mods, gains, up1, dn1, layer=l, sub=0, weight=0.5, **cx)
        q_c, k_c, v_c, sg_c, b_c, pool_c, conv_c = _inproj(cf, mods, gains, w_in_p, wgk_p, bgk_p, layer=l, **cx)
        q_l, k_l, v_l, sg_l, b_l, pool_l, conv_l = _inproj(xf, mods, gains, w_in_p, wgk_p, bgk_p, layer=l, **lat)
        zero = jnp.zeros((batch, 2, GLA_HEAD_V, GLA_DK), F32)
        of_c, ob_c, s_c = _gla(q_c, k_c, v_c, b_c, zero, batch=batch, tile=ctx_len)
        of_l, ob_l, _ = _gla(q_l, k_l, v_l, b_l, s_c, batch=batch, tile=gla_tile)
        pc_l = _poolconv(pool_l, conv_l, pc_weights, seq=seq, grid2d=True)
        xf = _mixffn(xf, of_l, ob_l, sg_l, pc_l, mods, gains, gng, wo, up2, dn2, layer=l, **lat)
        if not last:
            pc_c = _poolconv(pool_c, conv_c, pc_weights, seq=ctx_len, grid2d=False)
            cf = _mixffn(cf, of_c, ob_c, sg_c, pc_c, mods, gains, gng, wo, up2, dn2, layer=l, **cx)
    return xf.reshape(batch, seq, d)
```

```python
import functools
import math

import numpy as np
import jax
import jax.numpy as jnp
from jax import lax
from jax.experimental import pallas as pl
from jax.experimental.pallas import tpu as pltpu

EPS = 1e-6
GRID_W = 64
N_MOD = 9
GLA_HEADS = 4
GLA_HEAD_K = 64
GLA_HEAD_V = 128
GLA_DK = GLA_HEADS * GLA_HEAD_K
GLA_DV = GLA_HEADS * GLA_HEAD_V
GLA_LOWRANK = 16
GATE_NORM = 16.0
CHUNK = 64
LANES = 128
ROWS = 256
POOL_W = 256
POOL_HALF = (1, 2, 4, 8)
POOL_GROUP = 64
CONV_W = 256
CONV_K = 31
PROJ_BLOCK = 256
LR_PAD = 128
VMEM_LIMIT = 56 * 1024 * 1024

BF16 = jnp.bfloat16
F32 = jnp.float32


def _cparams(sem):
    return pltpu.CompilerParams(dimension_semantics=sem, vmem_limit_bytes=VMEM_LIMIT)


def _const_spec(shape):
    nd = len(shape)
    return pl.BlockSpec(shape, lambda *_: (0,) * nd, pipeline_mode=pl.Buffered(1))


def _rms(x, g):
    return x * lax.rsqrt(jnp.mean(x * x, axis=-1, keepdims=True) + EPS) * g


def _sub_in(x, mod_ref, gain_ref, i):
    shift = mod_ref[3 * i:3 * i + 1, :]
    scale = mod_ref[3 * i + 1:3 * i + 2, :]
    return _rms(x, gain_ref[2 * i:2 * i + 1, :]) * (1.0 + scale) + shift


def _sub_out(x, y, mod_ref, gain_ref, i, weight):
    gate = mod_ref[3 * i + 2:3 * i + 3, :]
    return x + weight * gate * _rms(y, gain_ref[2 * i + 1:2 * i + 2, :])


def _dot(a, b):
    return jnp.dot(a, b, preferred_element_type=F32)


def _row_of_tile(tiles_per_batch, fixed_row):
    if fixed_row is not None:
        return lambda t: fixed_row
    return lambda t: t // tiles_per_batch


def _row_groups(n):
    return [slice(r, r + ROWS) for r in range(0, n, ROWS)]


def _mods_kernel(c_ref, w_ref, b_ref, o_ref):
    s = jax.nn.silu(c_ref[...]).astype(BF16)
    o_ref[...] = _dot(s, w_ref[...].astype(BF16)) + b_ref[...]


def _mods(cc, w_ada, b_ada):
    depth, d, nd = w_ada.shape
    rows = cc.shape[0]
    n_col = nd // d
    return pl.pallas_call(
        _mods_kernel,
        grid=(depth, n_col),
        in_specs=[pl.BlockSpec((rows, d), lambda l, j: (0, 0)),
                  pl.BlockSpec((None, d, d), lambda l, j: (l, 0, j)),
                  pl.BlockSpec((None, 1, d), lambda l, j: (l, 0, j))],
        out_specs=pl.BlockSpec((None, rows, d), lambda l, j: (l, 0, j)),
        out_shape=jax.ShapeDtypeStruct((depth, rows, nd), F32),
        compiler_params=_cparams(("arbitrary", "arbitrary")),
        name="mods",
    )(cc, w_ada, b_ada.reshape(depth, 1, nd))


def _ffn_rows(x, mod_ref, gain_ref, wup_ref, wdn_ref, sub, weight):
    d_ff = wdn_ref.shape[0]
    h = _sub_in(x, mod_ref, gain_ref, sub).astype(BF16)
    a = _dot(h, wup_ref[:, :d_ff])
    b = _dot(h, wup_ref[:, d_ff:])
    hh = (jax.nn.silu(a) * b).astype(BF16)
    y = _dot(hh, wdn_ref[...])
    return _sub_out(x, y, mod_ref, gain_ref, sub, weight)


def _ffn_kernel(x_ref, mod_ref, gain_ref, wup_ref, wdn_ref, o_ref, *, sub, weight):
    for rows in _row_groups(x_ref.shape[0]):
        o_ref[rows, :] = _ffn_rows(x_ref[rows, :], mod_ref, gain_ref, wup_ref, wdn_ref, sub, weight)


def _ffn(x, mods, gains, w_up, w_dn, *, layer, sub, weight, tile, tiles_per_batch, fixed_row):
    n, d = x.shape
    row = _row_of_tile(tiles_per_batch, fixed_row)
    return pl.pallas_call(
        functools.partial(_ffn_kernel, sub=sub, weight=weight),
        grid=(n // tile,),
        in_specs=[pl.BlockSpec((tile, d), lambda t: (t, 0)),
                  pl.BlockSpec((None, None, N_MOD, d), lambda t: (layer, row(t), 0, 0)),
                  _const_spec(gains.shape),
                  _const_spec(w_up.shape),
                  _const_spec(w_dn.shape)],
        out_specs=pl.BlockSpec((tile, d), lambda t: (t, 0)),
        out_shape=jax.ShapeDtypeStruct((n, d), F32),
        compiler_params=_cparams(("parallel",)),
        name=f"ffn{sub}",
    )(x, mods, gains, w_up, w_dn)


def _log_sigmoid(x):
    return jnp.minimum(x, 0.0) - jnp.log(1.0 + jnp.exp(-jnp.abs(x)))


def _order_after(ref_tile, val):
    half = jnp.uint32(16)
    zero = lax.shift_right_logical(lax.shift_right_logical(pltpu.bitcast(val, jnp.uint32), half), half)
    ref_tile[...] = ref_tile[...] + pltpu.bitcast(zero, F32).astype(ref_tile.dtype)


def _proj_rows(x, rows, mod_ref, gain_ref, w_ref, wgk_ref, bgk_ref, tri_ref,
               q_ref, k_ref, v_ref, sg_ref, b_ref, pool_ref, conv_ref, h_scr=None, between=lambda: None):
    h = _sub_in(x, mod_ref, gain_ref, 1).astype(BF16)
    if h_scr is not None:
        h_scr[...] = h
    col = [0]

    def block(width):
        lhs = h if h_scr is None else h_scr[...]
        p = _dot(lhs, w_ref[:, col[0]:col[0] + width])
        col[0] += width
        return p

    def step():
        val = between()
        if val is not None:
            _order_after(h_scr.at[0:16, 0:LANES], val)

    def put(ref, c0, val):
        ref[rows, c0:c0 + val.shape[1]] = val.astype(ref.dtype)
        step()

    put(q_ref, 0, block(GLA_DK) * (GLA_HEAD_K ** -0.5))
    put(k_ref, 0, block(GLA_DK))
    for c0 in range(0, GLA_DV, PROJ_BLOCK):
        put(v_ref, c0, block(PROJ_BLOCK))
    for c0 in range(0, GLA_DV, PROJ_BLOCK):
        put(sg_ref, c0, jax.nn.silu(block(PROJ_BLOCK)))
    put(pool_ref, 0, block(POOL_W))
    conv_a = block(CONV_W)
    step()
    put(conv_ref, 0, conv_a * jax.nn.sigmoid(block(CONV_W)))
    logit = _dot(block(LR_PAD).astype(BF16), wgk_ref[...]) + bgk_ref[...]
    lg = _log_sigmoid(logit) * (math.log2(math.e) / GATE_NORM)
    hi = lg.astype(BF16)
    lo = (lg - hi.astype(F32)).astype(BF16)
    for d in range(2):
        cols = slice(d * GLA_DK, (d + 1) * GLA_DK)
        b_ref[d, rows, :] = _dot(tri_ref[d], hi[:, cols]) + _dot(tri_ref[d], lo[:, cols])
        step()


def _inproj_kernel(x_ref, mod_ref, gain_ref, w_ref, wgk_ref, bgk_ref, tri_ref,
                   q_ref, k_ref, v_ref, sg_ref, b_ref, pool_ref, conv_ref):
    for rows in _row_groups(x_ref.shape[0]):
        _proj_rows(x_ref[rows, :], rows, mod_ref, gain_ref, w_ref, wgk_ref, bgk_ref, tri_ref,
                   q_ref, k_ref, v_ref, sg_ref, b_ref, pool_ref, conv_ref)


def _chunk_tri():
    i = np.arange(ROWS)[:, None]
    j = np.arange(ROWS)[None, :]
    same = (i // CHUNK) == (j // CHUNK)
    return jnp.asarray(np.stack([same & (j <= i), same & (j >= i)]).astype(np.float32), BF16)


def _proj_out(n):
    return [jax.ShapeDtypeStruct((n, GLA_DK), BF16), jax.ShapeDtypeStruct((n, GLA_DK), BF16),
            jax.ShapeDtypeStruct((n, GLA_DV), BF16), jax.ShapeDtypeStruct((n, GLA_DV), BF16),
            jax.ShapeDtypeStruct((2, n, GLA_DK), F32)]


def _inproj(x, mods, gains, proj_w, *, layer, tile, tiles_per_batch, fixed_row):
    n, d = x.shape
    row = _row_of_tile(tiles_per_batch, fixed_row)
    tok = lambda width: pl.BlockSpec((tile, width), lambda t: (t, 0))
    return pl.pallas_call(
        _inproj_kernel,
        grid=(n // tile,),
        in_specs=[tok(d),
                  pl.BlockSpec((None, None, N_MOD, d), lambda t: (layer, row(t), 0, 0)),
                  _const_spec(gains.shape)] + [_const_spec(w.shape) for w in proj_w],
        out_specs=[tok(GLA_DK), tok(GLA_DK), tok(GLA_DV), tok(GLA_DV),
                   pl.BlockSpec((2, tile, GLA_DK), lambda t: (0, t, 0)),
                   tok(POOL_W), tok(CONV_W)],
        out_shape=_proj_out(n) + [jax.ShapeDtypeStruct((n, POOL_W), F32), jax.ShapeDtypeStruct((n, CONV_W), F32)],
        compiler_params=_cparams(("parallel",)),
        name="inproj",
    )(x, mods, gains, *proj_w)


_NT = (((1,), (1,)), ((), ()))
_TN = (((0,), (0,)), ((), ()))


def _gla_direction(q_ref, k_ref, v_ref, b_ref, o_ref, state, *, bwd):
    n_chunks = q_ref.shape[0] // CHUNK
    lane_lo = lax.broadcasted_iota(jnp.int32, (1, LANES), 1) < GLA_HEAD_K
    row = lax.broadcasted_iota(jnp.int32, (CHUNK, CHUNK), 0)
    col = lax.broadcasted_iota(jnp.int32, (CHUNK, CHUNK), 1)
    tri = (col >= row) if bwd else (col <= row)
    tri2 = jnp.concatenate([tri, tri], axis=0)

    def stack(t):
        return jnp.concatenate([jnp.where(lane_lo, t, 0.0), jnp.where(lane_lo, 0.0, t)], axis=0).astype(BF16)

    local = []
    for c in range(n_chunks):
        rows = slice(c * CHUNK, (c + 1) * CHUNK)
        qc = q_ref[rows, :].astype(F32)
        kc = k_ref[rows, :].astype(F32)
        vc = v_ref[rows, :]
        bc = b_ref[rows, :]
        b_mid = bc[CHUNK // 2:CHUNK // 2 + 1, :]
        b_tot = bc[0:1, :] if bwd else bc[CHUNK - 1:CHUNK, :]
        q_rel = qc * jnp.exp2(bc - b_mid)
        k_rel = (kc * jnp.exp2(b_mid - bc)).astype(BF16)
        q_in = qc * jnp.exp2(bc)
        k_end = kc * jnp.exp2(b_tot - bc)
        decay = jnp.exp2(b_tot)
        per_tile = []
        for t in range(GLA_DK // LANES):
            ls = slice(t * LANES, (t + 1) * LANES)
            att = lax.dot_general(stack(q_rel[:, ls]), k_rel[:, ls], _NT, preferred_element_type=F32)
            att = jnp.where(tri2, att, 0.0).astype(BF16)
            v_pair = [vc[:, (2 * t + hh) * GLA_HEAD_V:(2 * t + hh + 1) * GLA_HEAD_V] for hh in range(2)]
            o_intra = [_dot(att[hh * CHUNK:(hh + 1) * CHUNK, :], v_pair[hh]) for hh in range(2)]
            upd = lax.dot_general(jnp.concatenate(v_pair, axis=0), stack(k_end[:, ls]), _TN,
                                  preferred_element_type=F32)
            per_tile.append((stack(q_in[:, ls]), o_intra, upd, decay[:, ls]))
        local.append(per_tile)

    for c in (reversed(range(n_chunks)) if bwd else range(n_chunks)):
        outs = []
        for t, (q_in_st, o_intra, upd, decay) in enumerate(local[c]):
            o_inter = lax.dot_general(q_in_st, state[t].astype(BF16), _NT, preferred_element_type=F32)
            outs += [o_intra[hh] + o_inter[hh * CHUNK:(hh + 1) * CHUNK, :] for hh in range(2)]
            state[t] = state[t] * decay + upd
        o_ref[c * CHUNK:(c + 1) * CHUNK, :] = jnp.concatenate(outs, axis=1).astype(o_ref.dtype)
    return state


def _gla_kernel(qf_ref, kf_ref, vf_ref, bf_ref, qb_ref, kb_ref, vb_ref, bb_ref, s0_ref,
                of_ref, ob_ref, s_ref):
    @pl.when(pl.program_id(1) == 0)
    def _():
        s_ref[...] = s0_ref[...]

    n_lane_tiles = GLA_DK // LANES
    dirs = ((qf_ref, kf_ref, vf_ref, bf_ref, of_ref), (qb_ref, kb_ref, vb_ref, bb_ref, ob_ref))
    for d, refs in enumerate(dirs):
        state = [s_ref[d, :, t * LANES:(t + 1) * LANES] for t in range(n_lane_tiles)]
        state = _gla_direction(*refs, state, bwd=(d == 1))
        for t in range(n_lane_tiles):
            s_ref[d, :, t * LANES:(t + 1) * LANES] = state[t]


def _gla(q, k, v, b, s0, *, batch, tile):
    n = q.shape[0]
    nt = n // batch // tile
    fwd = lambda i, t: i * nt + t
    bwd = lambda i, t: i * nt + nt - 1 - t
    tok = lambda width, at: pl.BlockSpec((tile, width), lambda i, t: (at(i, t), 0))
    dec = lambda d, at: pl.BlockSpec((None, tile, GLA_DK), lambda i, t: (d, at(i, t), 0))
    state = pl.BlockSpec((None, 2, GLA_HEAD_V, GLA_DK), lambda i, t: (i, 0, 0, 0))
    return pl.pallas_call(
        _gla_kernel,
        grid=(batch, nt),
        in_specs=[tok(GLA_DK, fwd), tok(GLA_DK, fwd), tok(GLA_DV, fwd), dec(0, fwd),
                  tok(GLA_DK, bwd), tok(GLA_DK, bwd), tok(GLA_DV, bwd), dec(1, bwd),
                  state],
        out_specs=[tok(GLA_DV, fwd), tok(GLA_DV, bwd), state],
        out_shape=[jax.ShapeDtypeStruct((n, GLA_DV), BF16),
                   jax.ShapeDtypeStruct((n, GLA_DV), BF16),
                   jax.ShapeDtypeStruct((batch, 2, GLA_HEAD_V, GLA_DK), F32)],
        compiler_params=_cparams(("parallel", "arbitrary")),
        name="gla",
    )(q, k, v, b, q, k, v, b, s0)


POOL_PIECE = 128
CONV_PIECE = 64
CONV_PAD = 16


def _lane_half_width(first_group):
    lane_hi = lax.broadcasted_iota(jnp.int32, (1, LANES), 1) >= POOL_GROUP
    return jnp.where(lane_hi, POOL_HALF[first_group + 1], POOL_HALF[first_group])


def _window_sums(x, pos, period, halves, unit):
    n = x.shape[0]
    lane_hi = lax.broadcasted_iota(jnp.int32, (1, LANES), 1) >= POOL_GROUP
    trail = {1: x}
    lead = {1: x}
    w = 1
    while w < max(halves):
        trail[2 * w] = trail[w] + jnp.where(pos >= w, pltpu.roll(trail[w], w * unit, axis=0), 0.0)
        lead[2 * w] = lead[w] + jnp.where(pos < period - w, pltpu.roll(lead[w], n - w * unit, axis=0), 0.0)
        w *= 2
    t_sel = jnp.where(lane_hi, trail[halves[1]], trail[halves[0]])
    l_sel = jnp.where(lane_hi, lead[halves[1]], lead[halves[0]])
    return jnp.where(pos >= 1, pltpu.roll(t_sel, unit, axis=0), 0.0) + l_sel


def _window_count(pos, period, half):
    return (jnp.minimum(pos + half, period) - jnp.maximum(pos - half, 0)).astype(F32)


def _pool_tile_piece(ring, slots, r0, g, row0, n_rows):
    t = ring.shape[1]
    ls = slice(g * LANES, (g + 1) * LANES)
    lane_hi = lax.broadcasted_iota(jnp.int32, (1, LANES), 1) >= POOL_GROUP
    h_in, h_out = POOL_HALF[2 * g], POOL_HALF[2 * g + 1]

    def image_row(i):
        which, off = divmod(i * GRID_W + t, t)
        return ring[slots[which], off:off + GRID_W, ls]

    sums = []
    for j in range(r0 // GRID_W, (r0 + POOL_PIECE) // GRID_W):
        rowp = row0 + j

        def term(r):
            ok = (rowp >= -r) if r < 0 else (rowp < n_rows - r)
            return jnp.where(ok, image_row(j + r), 0.0)

        inner = sum(term(r) for r in range(-h_in, h_in))
        outer = sum(term(r) for r in list(range(-h_out, -h_in)) + list(range(h_in, h_out)))
        cnt = _window_count(rowp, n_rows, _lane_half_width(2 * g))
        sums.append((inner + jnp.where(lane_hi, outer, 0.0)) / cnt)
    s = jnp.concatenate(sums, axis=0)
    colp = lax.broadcasted_iota(jnp.int32, (POOL_PIECE, LANES), 0) % GRID_W
    s = _window_sums(s, colp, GRID_W, POOL_HALF[2 * g:2 * g + 2], 1)
    mean = s / _window_count(colp, GRID_W, _lane_half_width(2 * g))
    return mean, ring[slots[1], r0:r0 + POOL_PIECE, ls]


def _pool_seq_1d(x, period):
    n = x.shape[0]
    pos = lax.broadcasted_iota(jnp.int32, (n, LANES), 0) % period
    means = []
    for g in range(POOL_W // LANES):
        s = _window_sums(x[:, g * LANES:(g + 1) * LANES], pos, period, POOL_HALF[2 * g:2 * g + 2], 1)
        means.append(s / _window_count(pos, period, _lane_half_width(2 * g)))
    return jnp.concatenate(means, axis=1)


def _conv_act(big, dw_ref, dwb_ref, lng_ref, lnb_ref):
    total = big.shape[0]
    n = total - 2 * CONV_PAD
    acc = None
    for sft in range(8):
        shifted = big if sft == 0 else pltpu.roll(big, total - sft, axis=0)
        for a in range(4):
            off = 8 * a + sft
            if off < 1 or off > CONV_K:
                continue
            term = shifted[8 * a:8 * a + n, :] * dw_ref[off - 1:off, :]
            acc = term if acc is None else acc + term
    hc = acc + dwb_ref[...]
    mu = jnp.mean(hc, axis=-1, keepdims=True)
    var = jnp.mean(jnp.square(hc - mu), axis=-1, keepdims=True)
    hn = (hc - mu) * lax.rsqrt(var + EPS) * lng_ref[...] + lnb_ref[...]
    return jax.nn.silu(hn).astype(BF16)


def _poolconv_seq_kernel(pool_ref, conv_ref, pw_ref, ps_ref, dw_ref, dwb_ref, lng_ref, lnb_ref, cpw_ref, cpb_ref,
                         o_ref, *, period):
    n = pool_ref.shape[0]
    x = pool_ref[...]
    diff = (_pool_seq_1d(x, period) - x).astype(BF16)
    o_ref[:, :POOL_W] = (_dot(diff, pw_ref[...]) * ps_ref[...]).astype(o_ref.dtype)
    halo = jnp.zeros((CONV_PAD, CONV_W), F32)
    for s0 in range(0, n, period):
        big = jnp.concatenate([halo, conv_ref[s0:s0 + period, :], halo], axis=0)
        act = _conv_act(big, dw_ref, dwb_ref, lng_ref, lnb_ref)
        o_ref[s0:s0 + period, POOL_W:] = (_dot(act, cpw_ref[...]) + cpb_ref[...]).astype(o_ref.dtype)


def _poolconv_seq(pool_in, conv_in, weights, *, tile, period):
    n = pool_in.shape[0]
    tok = lambda width: pl.BlockSpec((tile, width), lambda t: (t, 0))
    return pl.pallas_call(
        functools.partial(_poolconv_seq_kernel, period=period),
        grid=(n // tile,),
        in_specs=[tok(POOL_W), tok(CONV_W)] + [_const_spec(w.shape) for w in weights],
        out_specs=tok(POOL_W + CONV_W),
        out_shape=jax.ShapeDtypeStruct((n, POOL_W + CONV_W), BF16),
        compiler_params=_cparams(("parallel",)),
        name="poolconv_seq",
    )(pool_in, conv_in, *weights)


RING = 3
LAG = 2


def _projpool_kernel(x_ref, mod_ref, gain_ref, w_ref, wgk_ref, bgk_ref, tri_ref,
                     pw_ref, ps_ref, dw_ref, dwb_ref, lng_ref, lnb_ref, cpw_ref, cpb_ref,
                     q_ref, k_ref, v_ref, sg_ref, b_ref, pc_ref,
                     pool_ring, conv_ring, pool_stage, conv_stage, diff_scr, act_scr, h_scr, *, tiles_per_batch, n_rows):
    s = pl.program_id(0)
    t = x_ref.shape[0]

    @pl.when(s == 0)
    def _():
        pool_ring[...] = jnp.zeros(pool_ring.shape, F32)
        conv_ring[...] = jnp.zeros(conv_ring.shape, F32)
        pool_stage[...] = jnp.zeros(pool_stage.shape, F32)
        conv_stage[...] = jnp.zeros(conv_stage.shape, F32)

    slot = lambda i: lax.rem(i + 2 * RING, RING)
    pool_ring[slot(s - 1)] = pool_stage[...]
    conv_ring[slot(s - 1)] = conv_stage[...]

    m = s - LAG
    in_batch = lax.rem(m + tiles_per_batch, tiles_per_batch)
    slots = (slot(m - 1), slot(m), slot(m + 1))
    row0 = in_batch * (t // GRID_W)
    has_before = in_batch > 0
    has_after = in_batch < tiles_per_batch - 1

    def pool_piece(r0, g):
        def run():
            mean, x = _pool_tile_piece(pool_ring, slots, r0, g, row0, n_rows)
            diff = mean - x
            diff_scr[r0:r0 + POOL_PIECE, g * LANES:(g + 1) * LANES] = diff.astype(BF16)
            return diff[0:16, :]
        return run

    def conv_piece(r0):
        def run():
            if r0 == 0:
                head = jnp.where(has_before, conv_ring[slots[0], t - CONV_PAD:t, :], 0.0)
                big = jnp.concatenate([head, conv_ring[slots[1], 0:CONV_PIECE + CONV_PAD, :]], axis=0)
            elif r0 + CONV_PIECE == t:
                tail = jnp.where(has_after, conv_ring[slots[2], 0:CONV_PAD, :], 0.0)
                big = jnp.concatenate([conv_ring[slots[1], r0 - CONV_PAD:t, :], tail], axis=0)
            else:
                big = conv_ring[slots[1], r0 - CONV_PAD:r0 + CONV_PIECE + CONV_PAD, :]
            act = _conv_act(big, dw_ref, dwb_ref, lng_ref, lnb_ref)
            act_scr[r0:r0 + CONV_PIECE, :] = act
            return act[0:16, 0:LANES].astype(F32)
        return run

    pieces = []
    conv_items = [conv_piece(r0) for r0 in range(0, t, CONV_PIECE)]
    pool_items = [pool_piece(r0, g) for r0 in range(0, t, POOL_PIECE) for g in range(POOL_W // LANES)]
    while conv_items or pool_items:
        if conv_items:
            pieces.append(conv_items.pop(0))
        if pool_items:
            pieces.append(pool_items.pop(0))

    def between():
        return pieces.pop(0)() if pieces else None

    for i, rows in enumerate(_row_groups(t)):
        _proj_rows(x_ref[rows, :], rows, mod_ref, gain_ref, w_ref, wgk_ref, bgk_ref, tri_ref,
                   q_ref, k_ref, v_ref, sg_ref, b_ref, pool_stage, conv_stage, h_scr.at[i], between)
    while pieces:
        between()
    pc_ref[:, :POOL_W] = (_dot(diff_scr[...], pw_ref[...]) * ps_ref[...]).astype(pc_ref.dtype)
    pc_ref[:, POOL_W:] = (_dot(act_scr[...], cpw_ref[...]) + cpb_ref[...]).astype(pc_ref.dtype)


def _projpool(x, mods, gains, proj_w, pc_w, *, layer, tile, tiles_per_batch, seq):
    n, d = x.shape
    n_tiles = n // tile
    assert tile % GRID_W == 0 and tile >= GRID_W * POOL_HALF[-1] and tiles_per_batch >= 1
    cur = lambda s: jnp.minimum(s, n_tiles - 1)
    tok = lambda width: pl.BlockSpec((tile, width), lambda s: (cur(s), 0))
    return pl.pallas_call(
        functools.partial(_projpool_kernel, tiles_per_batch=tiles_per_batch, n_rows=seq // GRID_W),
        grid=(n_tiles + LAG,),
        in_specs=[tok(d),
                  pl.BlockSpec((None, None, N_MOD, d), lambda s: (layer, cur(s) // tiles_per_batch, 0, 0)),
                  _const_spec(gains.shape)] + [_const_spec(w.shape) for w in proj_w + pc_w],
        out_specs=[tok(GLA_DK), tok(GLA_DK), tok(GLA_DV), tok(GLA_DV),
                   pl.BlockSpec((2, tile, GLA_DK), lambda s: (0, cur(s), 0)),
                   pl.BlockSpec((tile, POOL_W + CONV_W), lambda s: (jnp.maximum(s - LAG, 0), 0))],
        out_shape=_proj_out(n) + [jax.ShapeDtypeStruct((n, POOL_W + CONV_W), BF16)],
        scratch_shapes=[pltpu.VMEM((RING, tile, POOL_W), F32), pltpu.VMEM((RING, tile, CONV_W), F32),
                        pltpu.VMEM((tile, POOL_W), F32), pltpu.VMEM((tile, CONV_W), F32),
                        pltpu.VMEM((tile, POOL_W), BF16), pltpu.VMEM((tile, CONV_W), BF16),
                        pltpu.VMEM((tile // ROWS, ROWS, d), BF16)],
        compiler_params=_cparams(("arbitrary",)),
        name="projpool",
    )(x, mods, gains, *proj_w, *pc_w)


def _mixffn_kernel(x_ref, of_ref, ob_ref, sg_ref, pc_ref, mod_ref, gain_ref, gng_ref, wout_ref, wup_ref, wdn_ref,
                   out_ref):
    for rows in _row_groups(x_ref.shape[0]):
        o = of_ref[rows, :].astype(F32) + ob_ref[rows, :].astype(F32)
        sg = sg_ref[rows, :].astype(F32)
        ys = []
        for h in range(GLA_HEADS):
            hs = slice(h * GLA_HEAD_V, (h + 1) * GLA_HEAD_V)
            ys.append((_rms(o[:, hs], gng_ref[...]) * sg[:, hs]).astype(BF16))
        y_gla = jnp.concatenate(ys, axis=1)
        y = _dot(y_gla, wout_ref[:GLA_DV, :]) + _dot(pc_ref[rows, :], wout_ref[GLA_DV:, :])
        x = _sub_out(x_ref[rows, :], y, mod_ref, gain_ref, 1, 1.0)
        out_ref[rows, :] = _ffn_rows(x, mod_ref, gain_ref, wup_ref, wdn_ref, 2, 0.5)


def _mixffn(x, o_f, o_b, sg, pc, mods, gains, gng, w_out, w_up, w_dn, *, layer, tile, tiles_per_batch, fixed_row):
    n, d = x.shape
    row = _row_of_tile(tiles_per_batch, fixed_row)
    tok = lambda width: pl.BlockSpec((tile, width), lambda t: (t, 0))
    return pl.pallas_call(
        _mixffn_kernel,
        grid=(n // tile,),
        in_specs=[tok(d), tok(GLA_DV), tok(GLA_DV), tok(GLA_DV), tok(POOL_W + CONV_W),
                  pl.BlockSpec((None, None, N_MOD, d), lambda t: (layer, row(t), 0, 0)),
                  _const_spec(gains.shape),
                  _const_spec(gng.shape),
                  _const_spec(w_out.shape),
                  _const_spec(w_up.shape),
                  _const_spec(w_dn.shape)],
        out_specs=tok(d),
        out_shape=jax.ShapeDtypeStruct((n, d), F32),
        compiler_params=_cparams(("parallel",)),
        name="mixffn",
    )(x, o_f, o_b, sg, pc, mods, gains, gng, w_out, w_up, w_dn)


def _block_diag(blocks):
    g, a, b = blocks.shape
    out = jnp.zeros((g * a, g * b), blocks.dtype)
    for i in range(g):
        out = out.at[i * a:(i + 1) * a, i * b:(i + 1) * b].set(blocks[i])
    return out


def _pick_tile(n, want):
    t = min(n, want)
    assert n % t == 0 and t % ROWS == 0
    return t


def kernel(x, c, ctx, c_ctx, w_ada, b_ada, norm_g, ffn1_up, ffn1_down, ffn2_up, ffn2_down, w_in, w_gk2, b_gk,
           gla_norm_g, pool_w, pool_scale, conv_dw, conv_dw_b, conv_ln_g, conv_ln_b, conv_pw, conv_pw_b, w_out):
    batch, seq, d = x.shape
    ctx_len = ctx.shape[1]
    depth = w_ada.shape[0]
    assert seq % GRID_W == 0 and seq % CHUNK == 0 and ctx_len % CHUNK == 0

    xf = x.reshape(batch * seq, d)
    cf = ctx.reshape(batch * ctx_len, d)

    ctx_row = batch
    n_rows = -(-(batch + 1) // 8) * 8
    cc = jnp.zeros((n_rows, d), F32).at[:batch].set(c).at[ctx_row].set(c_ctx)
    mods = _mods(cc, w_ada, b_ada).reshape(depth, n_rows, N_MOD, d)

    lat_tile = _pick_tile(seq, 512)
    ctx_tile = _pick_tile(batch * ctx_len, 512)
    assert ctx_tile % ctx_len == 0
    lat = dict(tile=lat_tile, tiles_per_batch=seq // lat_tile, fixed_row=None)
    cx = dict(tile=ctx_tile, tiles_per_batch=None, fixed_row=ctx_row)
    gla_tile = _pick_tile(seq, 512)

    for l in range(depth):
        last = l == depth - 1
        gains = norm_g[l]
        up1, dn1 = ffn1_up[l].astype(BF16), ffn1_down[l].astype(BF16)
        up2, dn2 = ffn2_up[l].astype(BF16), ffn2_down[l].astype(BF16)
        wi = w_in[l]
        o_lr = 2 * GLA_DK + 2 * GLA_DV
        o_pool = o_lr + 2 * GLA_LOWRANK
        w_in_p = jnp.concatenate(
            [wi[:, :o_lr], wi[:, o_pool:], wi[:, o_lr:o_pool],
             jnp.zeros((d, LR_PAD - 2 * GLA_LOWRANK), F32)], axis=1).astype(BF16)
        wgk_p = jnp.zeros((LR_PAD, 2 * GLA_DK), F32)
        wgk_p = wgk_p.at[:GLA_LOWRANK, :GLA_DK].set(w_gk2[l, 0])
        wgk_p = wgk_p.at[GLA_LOWRANK:2 * GLA_LOWRANK, GLA_DK:].set(w_gk2[l, 1]).astype(BF16)
        bgk_p = b_gk[l].reshape(1, 2 * GLA_DK)
        proj_w = (w_in_p, wgk_p, bgk_p, _chunk_tri())
        pc_w = (_block_diag(pool_w[l]).astype(BF16), pool_scale[l].reshape(1, POOL_W),
                jnp.concatenate([conv_dw[l], jnp.zeros((1, CONV_W), F32)], axis=0),
                conv_dw_b[l].reshape(1, CONV_W), conv_ln_g[l].reshape(1, CONV_W),
                conv_ln_b[l].reshape(1, CONV_W), conv_pw[l].astype(BF16), conv_pw_b[l].reshape(1, CONV_W))
        gng = gla_norm_g[l].reshape(1, GLA_HEAD_V)
        wo = w_out[l].astype(BF16)

        xf = _ffn(xf, mods, gains, up1, dn1, layer=l, sub=0, weight=0.5, **lat)
        cf = _ffn(cf, mods, gains, up1, dn1, layer=l, sub=0, weight=0.5, **cx)
        q_c, k_c, v_c, sg_c, b_c, pool_c, conv_c = _inproj(cf, mods, gains, proj_w, layer=l, **cx)
        q_l, k_l, v_l, sg_l, b_l, pc_l = _projpool(xf, mods, gains, proj_w, pc_w, layer=l, tile=lat_tile,
                                                   tiles_per_batch=seq // lat_tile, seq=seq)
        zero = jnp.zeros((batch, 2, GLA_HEAD_V, GLA_DK), F32)
        of_c, ob_c, s_c = _gla(q_c, k_c, v_c, b_c, zero, batch=batch, tile=ctx_len)
        of_l, ob_l, _ = _gla(q_l, k_l, v_l, b_l, s_c, batch=batch, tile=gla_tile)
        xf = _mixffn(xf, of_l, ob_l, sg_l, pc_l, mods, gains, gng, wo, up2, dn2, layer=l, **lat)
        if not last:
            pc_c = _poolconv_seq(pool_c, conv_c, pc_w, tile=ctx_tile, period=ctx_len)
            cf = _mixffn(cf, of_c, ob_c, sg_c, pc_c, mods, gains, gng, wo, up2, dn2, layer=l, **cx)
    return xf.reshape(batch, seq, d)
```

```python
import functools
import math

import numpy as np
import jax
import jax.numpy as jnp
from jax import lax
from jax.experimental import pallas as pl
from jax.experimental.pallas import tpu as pltpu

EPS = 1e-6
GRID_W = 64
N_MOD = 9
GLA_HEADS = 4
GLA_HEAD_K = 64
GLA_HEAD_V = 128
GLA_DK = GLA_HEADS * GLA_HEAD_K
GLA_DV = GLA_HEADS * GLA_HEAD_V
GLA_LOWRANK = 16
GATE_NORM = 16.0
CHUNK = 64
LANES = 128
ROWS = 256
TOKEN_TILE = 1024
POOL_W = 256
POOL_HALF = (1, 2, 4, 8)
POOL_GROUP = 64
CONV_W = 256
CONV_K = 31
LR_PAD = 128
VMEM_LIMIT = 56 * 1024 * 1024

BF16 = jnp.bfloat16
F32 = jnp.float32


def _cparams(sem):
    return pltpu.CompilerParams(dimension_semantics=sem, vmem_limit_bytes=VMEM_LIMIT)


def _const_spec(shape):
    nd = len(shape)
    return pl.BlockSpec(shape, lambda *_: (0,) * nd, pipeline_mode=pl.Buffered(1))


def _rms(x, g):
    return x * lax.rsqrt(jnp.mean(x * x, axis=-1, keepdims=True) + EPS) * g


def _sub_in(x, mod_ref, gain_ref, i):
    shift = mod_ref[3 * i:3 * i + 1, :]
    scale = mod_ref[3 * i + 1:3 * i + 2, :]
    return _rms(x, gain_ref[2 * i:2 * i + 1, :]) * (1.0 + scale) + shift


def _sub_out(x, y, mod_ref, gain_ref, i, weight):
    gate = mod_ref[3 * i + 2:3 * i + 3, :]
    return x + weight * gate * _rms(y, gain_ref[2 * i + 1:2 * i + 2, :])


def _dot(a, b):
    return jnp.dot(a, b, preferred_element_type=F32)


def _row_of_tile(tiles_per_batch, fixed_row):
    if fixed_row is not None:
        return lambda t: fixed_row
    return lambda t: t // tiles_per_batch


def _row_groups(n):
    return [slice(r, r + ROWS) for r in range(0, n, ROWS)]


def _mods_kernel(c_ref, w_ref, b_ref, o_ref):
    s = jax.nn.silu(c_ref[...]).astype(BF16)
    o_ref[...] = _dot(s, w_ref[...].astype(BF16)) + b_ref[...]


def _mods(cc, w_ada, b_ada):
    depth, d, nd = w_ada.shape
    rows = cc.shape[0]
    n_col = nd // d
    return pl.pallas_call(
        _mods_kernel,
        grid=(depth, n_col),
        in_specs=[pl.BlockSpec((rows, d), lambda l, j: (0, 0)),
                  pl.BlockSpec((None, d, d), lambda l, j: (l, 0, j)),
                  pl.BlockSpec((None, 1, d), lambda l, j: (l, 0, j))],
        out_specs=pl.BlockSpec((None, rows, d), lambda l, j: (l, 0, j)),
        out_shape=jax.ShapeDtypeStruct((depth, rows, nd), F32),
        compiler_params=_cparams(("arbitrary", "arbitrary")),
        name="mods",
    )(cc, w_ada, b_ada.reshape(depth, 1, nd))


def _ffn_rows(x, mod_ref, gain_ref, wup_ref, wdn_ref, sub, weight):
    d_ff = wdn_ref.shape[0]
    h = _sub_in(x, mod_ref, gain_ref, sub).astype(BF16)
    a = _dot(h, wup_ref[:, :d_ff])
    b = _dot(h, wup_ref[:, d_ff:])
    hh = (jax.nn.silu(a) * b).astype(BF16)
    y = _dot(hh, wdn_ref[...])
    return _sub_out(x, y, mod_ref, gain_ref, sub, weight)


def _ffn_kernel(x_ref, mod_ref, gain_ref, wup_ref, wdn_ref, o_ref, *, sub, weight):
    for rows in _row_groups(x_ref.shape[0]):
        o_ref[rows, :] = _ffn_rows(x_ref[rows, :], mod_ref, gain_ref, wup_ref, wdn_ref, sub, weight)


def _ffn(x, mods, gains, w_up, w_dn, *, layer, sub, weight, tile, tiles_per_batch, fixed_row):
    n, d = x.shape
    row = _row_of_tile(tiles_per_batch, fixed_row)
    return pl.pallas_call(
        functools.partial(_ffn_kernel, sub=sub, weight=weight),
        grid=(n // tile,),
        in_specs=[pl.BlockSpec((tile, d), lambda t: (t, 0)),
                  pl.BlockSpec((None, None, N_MOD, d), lambda t: (layer, row(t), 0, 0)),
                  _const_spec(gains.shape),
                  _const_spec(w_up.shape),
                  _const_spec(w_dn.shape)],
        out_specs=pl.BlockSpec((tile, d), lambda t: (t, 0)),
        out_shape=jax.ShapeDtypeStruct((n, d), F32),
        compiler_params=_cparams(("parallel",)),
        name=f"ffn{sub}",
    )(x, mods, gains, w_up, w_dn)


def _log_sigmoid(x):
    return jnp.minimum(x, 0.0) - jnp.log(1.0 + jnp.exp(-jnp.abs(x)))


def _inproj_kernel(x_ref, mod_ref, gain_ref, w_ref, wgk_ref, bgk_ref, tri_ref,
                   q_ref, k_ref, v_ref, sg_ref, b_ref, pool_ref, conv_ref):
    n_main = w_ref.shape[1] - LR_PAD
    for rows in _row_groups(x_ref.shape[0]):
        h = _sub_in(x_ref[rows, :], mod_ref, gain_ref, 1).astype(BF16)
        lr = _dot(h, w_ref[:, n_main:]).astype(BF16)
        logit = _dot(lr, wgk_ref[...]) + bgk_ref[...]
        lg = _log_sigmoid(logit) * (math.log2(math.e) / GATE_NORM)
        hi = lg.astype(BF16)
        lo = (lg - hi.astype(F32)).astype(BF16)
        p = _dot(h, w_ref[:, :n_main])
        for d in range(2):
            cols = slice(d * GLA_DK, (d + 1) * GLA_DK)
            b_ref[d, rows, :] = _dot(tri_ref[d], hi[:, cols]) + _dot(tri_ref[d], lo[:, cols])
        c0 = 0
        q_ref[rows, :] = (p[:, c0:c0 + GLA_DK] * (GLA_HEAD_K ** -0.5)).astype(BF16)
        c0 += GLA_DK
        k_ref[rows, :] = p[:, c0:c0 + GLA_DK].astype(BF16)
        c0 += GLA_DK
        v_ref[rows, :] = p[:, c0:c0 + GLA_DV].astype(BF16)
        c0 += GLA_DV
        sg_ref[rows, :] = jax.nn.silu(p[:, c0:c0 + GLA_DV]).astype(BF16)
        c0 += GLA_DV
        pool_ref[rows, :] = p[:, c0:c0 + POOL_W]
        c0 += POOL_W
        conv_ref[rows, :] = p[:, c0:c0 + CONV_W] * jax.nn.sigmoid(p[:, c0 + CONV_W:c0 + 2 * CONV_W])


def _chunk_tri():
    i = np.arange(ROWS)[:, None]
    j = np.arange(ROWS)[None, :]
    same = (i // CHUNK) == (j // CHUNK)
    return jnp.asarray(np.stack([same & (j <= i), same & (j >= i)]).astype(np.float32), BF16)


def _inproj(x, mods, gains, w_in_p, wgk_p, bgk_p, *, layer, tile, tiles_per_batch, fixed_row):
    n, d = x.shape
    row = _row_of_tile(tiles_per_batch, fixed_row)
    tok = lambda width: pl.BlockSpec((tile, width), lambda t: (t, 0))
    tri = _chunk_tri()
    return pl.pallas_call(
        _inproj_kernel,
        grid=(n // tile,),
        in_specs=[tok(d),
                  pl.BlockSpec((None, None, N_MOD, d), lambda t: (layer, row(t), 0, 0)),
                  _const_spec(gains.shape),
                  _const_spec(w_in_p.shape),
                  _const_spec(wgk_p.shape),
                  _const_spec(bgk_p.shape),
                  _const_spec(tri.shape)],
        out_specs=[tok(GLA_DK), tok(GLA_DK), tok(GLA_DV), tok(GLA_DV),
                   pl.BlockSpec((2, tile, GLA_DK), lambda t: (0, t, 0)),
                   tok(POOL_W), tok(CONV_W)],
        out_shape=[jax.ShapeDtypeStruct((n, GLA_DK), BF16),
                   jax.ShapeDtypeStruct((n, GLA_DK), BF16),
                   jax.ShapeDtypeStruct((n, GLA_DV), BF16),
                   jax.ShapeDtypeStruct((n, GLA_DV), BF16),
                   jax.ShapeDtypeStruct((2, n, GLA_DK), F32),
                   jax.ShapeDtypeStruct((n, POOL_W), F32),
                   jax.ShapeDtypeStruct((n, CONV_W), F32)],
        compiler_params=_cparams(("parallel",)),
        name="inproj",
    )(x, mods, gains, w_in_p, wgk_p, bgk_p, tri)


_NT = (((1,), (1,)), ((), ()))
_TN = (((0,), (0,)), ((), ()))


def _gla_direction(q_ref, k_ref, v_ref, b_ref, o_ref, state, *, bwd):
    n_chunks = q_ref.shape[0] // CHUNK
    lane_lo = lax.broadcasted_iota(jnp.int32, (1, LANES), 1) < GLA_HEAD_K
    row = lax.broadcasted_iota(jnp.int32, (CHUNK, CHUNK), 0)
    col = lax.broadcasted_iota(jnp.int32, (CHUNK, CHUNK), 1)
    tri = (col >= row) if bwd else (col <= row)
    tri2 = jnp.concatenate([tri, tri], axis=0)

    def stack(t):
        return jnp.concatenate([jnp.where(lane_lo, t, 0.0), jnp.where(lane_lo, 0.0, t)], axis=0).astype(BF16)

    local = []
    for c in range(n_chunks):
        rows = slice(c * CHUNK, (c + 1) * CHUNK)
        qc = q_ref[rows, :].astype(F32)
        kc = k_ref[rows, :].astype(F32)
        vc = v_ref[rows, :]
        bc = b_ref[rows, :]
        b_mid = bc[CHUNK // 2:CHUNK // 2 + 1, :]
        b_tot = bc[0:1, :] if bwd else bc[CHUNK - 1:CHUNK, :]
        q_rel = qc * jnp.exp2(bc - b_mid)
        k_rel = (kc * jnp.exp2(b_mid - bc)).astype(BF16)
        q_in = qc * jnp.exp2(bc)
        k_end = kc * jnp.exp2(b_tot - bc)
        decay = jnp.exp2(b_tot)
        per_tile = []
        for t in range(GLA_DK // LANES):
            ls = slice(t * LANES, (t + 1) * LANES)
            att = lax.dot_general(stack(q_rel[:, ls]), k_rel[:, ls], _NT, preferred_element_type=F32)
            att = jnp.where(tri2, att, 0.0).astype(BF16)
            v_pair = [vc[:, (2 * t + hh) * GLA_HEAD_V:(2 * t + hh + 1) * GLA_HEAD_V] for hh in range(2)]
            o_intra = [_dot(att[hh * CHUNK:(hh + 1) * CHUNK, :], v_pair[hh]) for hh in range(2)]
            upd = lax.dot_general(jnp.concatenate(v_pair, axis=0), stack(k_end[:, ls]), _TN,
                                  preferred_element_type=F32)
            per_tile.append((stack(q_in[:, ls]), o_intra, upd, decay[:, ls]))
        local.append(per_tile)

    for c in (reversed(range(n_chunks)) if bwd else range(n_chunks)):
        outs = []
        for t, (q_in_st, o_intra, upd, decay) in enumerate(local[c]):
            o_inter = lax.dot_general(q_in_st, state[t].astype(BF16), _NT, preferred_element_type=F32)
            outs += [o_intra[hh] + o_inter[hh * CHUNK:(hh + 1) * CHUNK, :] for hh in range(2)]
            state[t] = state[t] * decay + upd
        o_ref[c * CHUNK:(c + 1) * CHUNK, :] = jnp.concatenate(outs, axis=1).astype(o_ref.dtype)
    return state


def _gla_kernel(qf_ref, kf_ref, vf_ref, bf_ref, qb_ref, kb_ref, vb_ref, bb_ref, s0_ref,
                of_ref, ob_ref, s_ref):
    @pl.when(pl.program_id(1) == 0)
    def _():
        s_ref[...] = s0_ref[...]

    n_lane_tiles = GLA_DK // LANES
    dirs = ((qf_ref, kf_ref, vf_ref, bf_ref, of_ref), (qb_ref, kb_ref, vb_ref, bb_ref, ob_ref))
    for d, refs in enumerate(dirs):
        state = [s_ref[d, :, t * LANES:(t + 1) * LANES] for t in range(n_lane_tiles)]
        state = _gla_direction(*refs, state, bwd=(d == 1))
        for t in range(n_lane_tiles):
            s_ref[d, :, t * LANES:(t + 1) * LANES] = state[t]


def _gla(q, k, v, b, s0, *, batch, tile):
    n = q.shape[0]
    nt = n // batch // tile
    fwd = lambda i, t: i * nt + t
    bwd = lambda i, t: i * nt + nt - 1 - t
    tok = lambda width, at: pl.BlockSpec((tile, width), lambda i, t: (at(i, t), 0))
    dec = lambda d, at: pl.BlockSpec((None, tile, GLA_DK), lambda i, t: (d, at(i, t), 0))
    state = pl.BlockSpec((None, 2, GLA_HEAD_V, GLA_DK), lambda i, t: (i, 0, 0, 0))
    return pl.pallas_call(
        _gla_kernel,
        grid=(batch, nt),
        in_specs=[tok(GLA_DK, fwd), tok(GLA_DK, fwd), tok(GLA_DV, fwd), dec(0, fwd),
                  tok(GLA_DK, bwd), tok(GLA_DK, bwd), tok(GLA_DV, bwd), dec(1, bwd),
                  state],
        out_specs=[tok(GLA_DV, fwd), tok(GLA_DV, bwd), state],
        out_shape=[jax.ShapeDtypeStruct((n, GLA_DV), BF16),
                   jax.ShapeDtypeStruct((n, GLA_DV), BF16),
                   jax.ShapeDtypeStruct((batch, 2, GLA_HEAD_V, GLA_DK), F32)],
        compiler_params=_cparams(("parallel", "arbitrary")),
        name="gla",
    )(q, k, v, b, q, k, v, b, s0)


def _window_sums(x, pos, period, halves, unit):
    n = x.shape[0]
    lane_hi = lax.broadcasted_iota(jnp.int32, (1, LANES), 1) >= POOL_GROUP
    trail = {1: x}
    lead = {1: x}
    w = 1
    while w < max(halves):
        trail[2 * w] = trail[w] + jnp.where(pos >= w, pltpu.roll(trail[w], w * unit, axis=0), 0.0)
        lead[2 * w] = lead[w] + jnp.where(pos < period - w, pltpu.roll(lead[w], n - w * unit, axis=0), 0.0)
        w *= 2
    t_sel = jnp.where(lane_hi, trail[halves[1]], trail[halves[0]])
    l_sel = jnp.where(lane_hi, lead[halves[1]], lead[halves[0]])
    return jnp.where(pos >= 1, pltpu.roll(t_sel, unit, axis=0), 0.0) + l_sel


def _window_count(pos, period, half):
    return (jnp.minimum(pos + half, period) - jnp.maximum(pos - half, 0)).astype(F32)


def _poolconv_kernel(pool_ref, conv_ref, pw_ref, ps_ref, dw_ref, dwb_ref, lng_ref, lnb_ref, cpw_ref, cpb_ref,
                     o_ref, hpad_ref, *, grid2d):
    n = pool_ref.shape[0]
    x = pool_ref[...]
    tok = lax.broadcasted_iota(jnp.int32, (n, 1), 0)
    lane = lax.broadcasted_iota(jnp.int32, (1, POOL_W), 1)
    half = jnp.where(lane < 64, POOL_HALF[0], jnp.where(lane < 128, POOL_HALF[1],
                     jnp.where(lane < 192, POOL_HALF[2], POOL_HALF[3])))
    if grid2d:
        n_rows = n // GRID_W
        colp = tok % GRID_W
        rowp = tok // GRID_W
        s = jnp.concatenate([_window_sums(x[:, :128], rowp, n_rows, POOL_HALF[0:2], GRID_W),
                             _window_sums(x[:, 128:], rowp, n_rows, POOL_HALF[2:4], GRID_W)], axis=1)
        s = s / _window_count(rowp, n_rows, half)
        s = jnp.concatenate([_window_sums(s[:, :128], colp, GRID_W, POOL_HALF[0:2], 1),
                             _window_sums(s[:, 128:], colp, GRID_W, POOL_HALF[2:4], 1)], axis=1)
        m = s / _window_count(colp, GRID_W, half)
    else:
        s = jnp.concatenate([_window_sums(x[:, :128], tok, n, POOL_HALF[0:2], 1),
                             _window_sums(x[:, 128:], tok, n, POOL_HALF[2:4], 1)], axis=1)
        m = s / _window_count(tok, n, half)
    y_pool = _dot((m - x).astype(BF16), pw_ref[...]) * ps_ref[...]
    o_ref[:, :POOL_W] = y_pool.astype(o_ref.dtype)

    pad = 16
    hpad_ref[0:pad, :] = jnp.zeros((pad, CONV_W), F32)
    hpad_ref[pad + n:pad + n + pad, :] = jnp.zeros((pad, CONV_W), F32)
    hpad_ref[pad:pad + n, :] = conv_ref[...]
    big = hpad_ref[...]
    total = n + 2 * pad
    acc = jnp.zeros((n, CONV_W), F32)
    for sft in range(8):
        shifted = big if sft == 0 else pltpu.roll(big, total - sft, axis=0)
        for a in range(4):
            off = 8 * a + sft
            if off < 1 or off > CONV_K:
                continue
            acc = acc + shifted[8 * a:8 * a + n, :] * dw_ref[off - 1:off, :]
    hc = acc + dwb_ref[...]
    mu = jnp.mean(hc, axis=-1, keepdims=True)
    var = jnp.mean(jnp.square(hc - mu), axis=-1, keepdims=True)
    hn = (hc - mu) * lax.rsqrt(var + EPS) * lng_ref[...] + lnb_ref[...]
    y_conv = _dot(jax.nn.silu(hn).astype(BF16), cpw_ref[...]) + cpb_ref[...]
    o_ref[:, POOL_W:] = y_conv.astype(o_ref.dtype)


def _poolconv(pool_in, conv_in, weights, *, seq, grid2d):
    n = pool_in.shape[0]
    tok = lambda width: pl.BlockSpec((seq, width), lambda b: (b, 0))
    return pl.pallas_call(
        functools.partial(_poolconv_kernel, grid2d=grid2d),
        grid=(n // seq,),
        in_specs=[tok(POOL_W), tok(CONV_W)] + [_const_spec(w.shape) for w in weights],
        out_specs=tok(POOL_W + CONV_W),
        out_shape=jax.ShapeDtypeStruct((n, POOL_W + CONV_W), BF16),
        scratch_shapes=[pltpu.VMEM((seq + 32, CONV_W), F32)],
        compiler_params=_cparams(("parallel",)),
        name="poolconv",
    )(pool_in, conv_in, *weights)


def _mixffn_kernel(x_ref, of_ref, ob_ref, sg_ref, pc_ref, mod_ref, gain_ref, gng_ref, wout_ref, wup_ref, wdn_ref,
                   out_ref):
    for rows in _row_groups(x_ref.shape[0]):
        o = of_ref[rows, :].astype(F32) + ob_ref[rows, :].astype(F32)
        sg = sg_ref[rows, :].astype(F32)
        ys = []
        for h in range(GLA_HEADS):
            hs = slice(h * GLA_HEAD_V, (h + 1) * GLA_HEAD_V)
            ys.append((_rms(o[:, hs], gng_ref[...]) * sg[:, hs]).astype(BF16))
        y_gla = jnp.concatenate(ys, axis=1)
        y = _dot(y_gla, wout_ref[:GLA_DV, :]) + _dot(pc_ref[rows, :], wout_ref[GLA_DV:, :])
        x = _sub_out(x_ref[rows, :], y, mod_ref, gain_ref, 1, 1.0)
        out_ref[rows, :] = _ffn_rows(x, mod_ref, gain_ref, wup_ref, wdn_ref, 2, 0.5)


def _mixffn(x, o_f, o_b, sg, pc, mods, gains, gng, w_out, w_up, w_dn, *, layer, tile, tiles_per_batch, fixed_row):
    n, d = x.shape
    row = _row_of_tile(tiles_per_batch, fixed_row)
    tok = lambda width: pl.BlockSpec((tile, width), lambda t: (t, 0))
    return pl.pallas_call(
        _mixffn_kernel,
        grid=(n // tile,),
        in_specs=[tok(d), tok(GLA_DV), tok(GLA_DV), tok(GLA_DV), tok(POOL_W + CONV_W),
                  pl.BlockSpec((None, None, N_MOD, d), lambda t: (layer, row(t), 0, 0)),
                  _const_spec(gains.shape),
                  _const_spec(gng.shape),
                  _const_spec(w_out.shape),
                  _const_spec(w_up.shape),
                  _const_spec(w_dn.shape)],
        out_specs=tok(d),
        out_shape=jax.ShapeDtypeStruct((n, d), F32),
        compiler_params=_cparams(("parallel",)),
        name="mixffn",
    )(x, o_f, o_b, sg, pc, mods, gains, gng, w_out, w_up, w_dn)


def _block_diag(blocks):
    g, a, b = blocks.shape
    out = jnp.zeros((g * a, g * b), blocks.dtype)
    for i in range(g):
        out = out.at[i * a:(i + 1) * a, i * b:(i + 1) * b].set(blocks[i])
    return out


def _pick_tile(n, want):
    t = min(n, want)
    assert n % t == 0 and t % ROWS == 0
    return t


def kernel(x, c, ctx, c_ctx, w_ada, b_ada, norm_g, ffn1_up, ffn1_down, ffn2_up, ffn2_down, w_in, w_gk2, b_gk,
           gla_norm_g, pool_w, pool_scale, conv_dw, conv_dw_b, conv_ln_g, conv_ln_b, conv_pw, conv_pw_b, w_out):
    batch, seq, d = x.shape
    ctx_len = ctx.shape[1]
    depth = w_ada.shape[0]
    assert seq % GRID_W == 0 and seq % CHUNK == 0 and ctx_len % CHUNK == 0

    xf = x.reshape(batch * seq, d)
    cf = ctx.reshape(batch * ctx_len, d)

    ctx_row = batch
    n_rows = -(-(batch + 1) // 8) * 8
    cc = jnp.zeros((n_rows, d), F32).at[:batch].set(c).at[ctx_row].set(c_ctx)
    mods = _mods(cc, w_ada, b_ada).reshape(depth, n_rows, N_MOD, d)

    lat_tile = _pick_tile(seq, TOKEN_TILE)
    ctx_tile = _pick_tile(batch * ctx_len, TOKEN_TILE)
    lat = dict(tile=lat_tile, tiles_per_batch=seq // lat_tile, fixed_row=None)
    cx = dict(tile=ctx_tile, tiles_per_batch=None, fixed_row=ctx_row)
    gla_tile = _pick_tile(seq, TOKEN_TILE)

    for l in range(depth):
        last = l == depth - 1
        gains = norm_g[l]
        up1, dn1 = ffn1_up[l].astype(BF16), ffn1_down[l].astype(BF16)
        up2, dn2 = ffn2_up[l].astype(BF16), ffn2_down[l].astype(BF16)
        wi = w_in[l]
        o_lr = 2 * GLA_DK + 2 * GLA_DV
        o_pool = o_lr + 2 * GLA_LOWRANK
        w_in_p = jnp.concatenate(
            [wi[:, :o_lr], wi[:, o_pool:], wi[:, o_lr:o_pool],
             jnp.zeros((d, LR_PAD - 2 * GLA_LOWRANK), F32)], axis=1).astype(BF16)
        wgk_p = jnp.zeros((LR_PAD, 2 * GLA_DK), F32)
        wgk_p = wgk_p.at[:GLA_LOWRANK, :GLA_DK].set(w_gk2[l, 0])
        wgk_p = wgk_p.at[GLA_LOWRANK:2 * GLA_LOWRANK, GLA_DK:].set(w_gk2[l, 1]).astype(BF16)
        bgk_p = b_gk[l].reshape(1, 2 * GLA_DK)
        pc_weights = (_block_diag(pool_w[l]).astype(BF16), pool_scale[l].reshape(1, POOL_W),
                      jnp.concatenate([conv_dw[l], jnp.zeros((1, CONV_W), F32)], axis=0),
                      conv_dw_b[l].reshape(1, CONV_W), conv_ln_g[l].reshape(1, CONV_W),
                      conv_ln_b[l].reshape(1, CONV_W), conv_pw[l].astype(BF16), conv_pw_b[l].reshape(1, CONV_W))
        gng = gla_norm_g[l].reshape(1, GLA_HEAD_V)
        wo = w_out[l].astype(BF16)

        xf = _ffn(xf, mods, gains, up1, dn1, layer=l, sub=0, weight=0.5, **lat)
        cf = _ffn(cf, mods, gains, up1, dn1, layer=l, sub=0, weight=0.5, **cx)
        q_c, k_c, v_c, sg_c, b_c, pool_c, conv_c = _inproj(cf, mods, gains, w_in_p, wgk_p, bgk_p, layer=l, **cx)
        q_l, k_l, v_l, sg_l, b_l, pool_l, conv_l = _inproj(xf, mods, gains, w_in_p, wgk_p, bgk_p, layer=l, **lat)
        zero = jnp.zeros((batch, 2, GLA_HEAD_V, GLA_DK), F32)
        of_c, ob_c, s_c = _gla(q_c, k_c, v_c, b_c, zero, batch=batch, tile=ctx_len)
        of_l, ob_l, _ = _gla(q_l, k_l, v_l, b_l, s_c, batch=batch, tile=gla_tile)
        pc_l = _poolconv(pool_l, conv_l, pc_weights, seq=seq, grid2d=True)
        xf = _mixffn(xf, of_l, ob_l, sg_l, pc_l, mods, gains, gng, wo, up2, dn2, layer=l, **lat)
        if not last:
            pc_c = _poolconv(pool_c, conv_c, pc_weights, seq=ctx_len, grid2d=False)
            cf = _mixffn(cf, of_c, ob_c, sg_c, pc_c, mods, gains, gng, wo, up2, dn2, layer=l, **cx)
    return xf.reshape(batch, seq, d)
```

```python
import functools
import math

import numpy as np
import jax
import jax.numpy as jnp
from jax import lax
from jax.experimental import pallas as pl
from jax.experimental.pallas import tpu as pltpu

EPS = 1e-6
GRID_W = 64
N_MOD = 9
GLA_HEADS = 4
GLA_HEAD_K = 64
GLA_HEAD_V = 128
GLA_DK = GLA_HEADS * GLA_HEAD_K
GLA_DV = GLA_HEADS * GLA_HEAD_V
GLA_LOWRANK = 16
GATE_NORM = 16.0
CHUNK = 64
LANES = 128
ROWS = 256
TOKEN_TILE = 1024
POOL_W = 256
POOL_HALF = (1, 2, 4, 8)
POOL_GROUP = 64
CONV_W = 256
CONV_K = 31
LR_PAD = 128
VMEM_LIMIT = 56 * 1024 * 1024

BF16 = jnp.bfloat16
F32 = jnp.float32


def _cparams(sem):
    return pltpu.CompilerParams(dimension_semantics=sem, vmem_limit_bytes=VMEM_LIMIT)


def _const_spec(shape):
    nd = len(shape)
    return pl.BlockSpec(shape, lambda *_: (0,) * nd, pipeline_mode=pl.Buffered(1))


def _layer_spec(arr, layer):
    nd = arr.ndim - 1
    return pl.BlockSpec((None,) + arr.shape[1:], lambda *_: (layer,) + (0,) * nd, pipeline_mode=pl.Buffered(1))


def _rms(x, g):
    return x * lax.rsqrt(jnp.mean(x * x, axis=-1, keepdims=True) + EPS) * g


def _sub_in(x, mod_ref, gain_ref, i):
    shift = mod_ref[3 * i:3 * i + 1, :]
    scale = mod_ref[3 * i + 1:3 * i + 2, :]
    return _rms(x, gain_ref[2 * i:2 * i + 1, :]) * (1.0 + scale) + shift


def _sub_out(x, y, mod_ref, gain_ref, i, weight):
    gate = mod_ref[3 * i + 2:3 * i + 3, :]
    return x + weight * gate * _rms(y, gain_ref[2 * i + 1:2 * i + 2, :])


def _dot(a, b):
    return jnp.dot(a, b, preferred_element_type=F32)


def _row_of_tile(tiles_per_batch, fixed_row):
    if fixed_row is not None:
        return lambda t: fixed_row
    return lambda t: t // tiles_per_batch


def _row_groups(n):
    return [slice(r, r + ROWS) for r in range(0, n, ROWS)]


def _pipelined(n, pre, main):
    groups = _row_groups(n)
    staged = [pre(groups[0])]
    for i, rows in enumerate(groups):
        if i + 1 < len(groups):
            staged.append(pre(groups[i + 1]))
        main(rows, *staged[i])


def _mods_kernel(c_ref, w_ref, b_ref, o_ref):
    s = jax.nn.silu(c_ref[...]).astype(BF16)
    o_ref[...] = _dot(s, w_ref[...].astype(BF16)) + b_ref[...]


def _mods(cc, w_ada, b_ada):
    depth, d, nd = w_ada.shape
    rows = cc.shape[0]
    n_col = nd // d
    return pl.pallas_call(
        _mods_kernel,
        grid=(depth, n_col),
        in_specs=[pl.BlockSpec((rows, d), lambda l, j: (0, 0)),
                  pl.BlockSpec((None, d, d), lambda l, j: (l, 0, j)),
                  pl.BlockSpec((None, 1, d), lambda l, j: (l, 0, j))],
        out_specs=pl.BlockSpec((None, rows, d), lambda l, j: (l, 0, j)),
        out_shape=jax.ShapeDtypeStruct((depth, rows, nd), F32),
        compiler_params=_cparams(("arbitrary", "arbitrary")),
        name="mods",
    )(cc, w_ada, b_ada.reshape(depth, 1, nd))


def _ffn_pre(x, mod_ref, gain_ref, sub):
    return x, _sub_in(x, mod_ref, gain_ref, sub).astype(BF16)


def _ffn_main(x, h, mod_ref, gain_ref, wup_ref, wdn_ref, sub, weight):
    d_ff = wdn_ref.shape[0]
    a = _dot(h, wup_ref[:, :d_ff])
    b = _dot(h, wup_ref[:, d_ff:])
    hh = (jax.nn.silu(a) * b).astype(BF16)
    y = _dot(hh, wdn_ref[...])
    return _sub_out(x, y, mod_ref, gain_ref, sub, weight)


def _ffn_kernel(x_ref, mod_ref, gain_ref, wup_ref, wdn_ref, o_ref, *, sub, weight):
    def pre(rows):
        return _ffn_pre(x_ref[rows, :], mod_ref, gain_ref, sub)

    def main(rows, x, h):
        o_ref[rows, :] = _ffn_main(x, h, mod_ref, gain_ref, wup_ref, wdn_ref, sub, weight)

    _pipelined(x_ref.shape[0], pre, main)


def _ffn(x, mods, gains, w_up, w_dn, *, layer, sub, weight, tile, tiles_per_batch, fixed_row):
    n, d = x.shape
    row = _row_of_tile(tiles_per_batch, fixed_row)
    return pl.pallas_call(
        functools.partial(_ffn_kernel, sub=sub, weight=weight),
        grid=(n // tile,),
        in_specs=[pl.BlockSpec((tile, d), lambda t: (t, 0)),
                  pl.BlockSpec((None, None, N_MOD, d), lambda t: (layer, row(t), 0, 0)),
                  _layer_spec(gains, layer),
                  _layer_spec(w_up, layer),
                  _layer_spec(w_dn, layer)],
        out_specs=pl.BlockSpec((tile, d), lambda t: (t, 0)),
        out_shape=jax.ShapeDtypeStruct((n, d), F32),
        compiler_params=_cparams(("parallel",)),
        name=f"ffn{sub}",
    )(x, mods, gains, w_up, w_dn)


def _log_sigmoid(x):
    return jnp.minimum(x, 0.0) - jnp.log(1.0 + jnp.exp(-jnp.abs(x)))


def _inproj_kernel(x_ref, mod_ref, gain_ref, w_ref, wgk_ref, bgk_ref, tri_ref,
                   q_ref, k_ref, v_ref, sg_ref, b_ref, pool_ref, conv_ref):
    n_main = w_ref.shape[1] - LR_PAD

    def pre(rows):
        return (_sub_in(x_ref[rows, :], mod_ref, gain_ref, 1).astype(BF16),)

    def main(rows, h):
        lr = _dot(h, w_ref[:, n_main:]).astype(BF16)
        logit = _dot(lr, wgk_ref[...]) + bgk_ref[...]
        lg = _log_sigmoid(logit) * (math.log2(math.e) / GATE_NORM)
        hi = lg.astype(BF16)
        lo = (lg - hi.astype(F32)).astype(BF16)
        p = _dot(h, w_ref[:, :n_main])
        for d in range(2):
            cols = slice(d * GLA_DK, (d + 1) * GLA_DK)
            b_ref[d, rows, :] = _dot(tri_ref[d], hi[:, cols]) + _dot(tri_ref[d], lo[:, cols])
        c0 = 0
        q_ref[rows, :] = (p[:, c0:c0 + GLA_DK] * (GLA_HEAD_K ** -0.5)).astype(BF16)
        c0 += GLA_DK
        k_ref[rows, :] = p[:, c0:c0 + GLA_DK].astype(BF16)
        c0 += GLA_DK
        v_ref[rows, :] = p[:, c0:c0 + GLA_DV].astype(BF16)
        c0 += GLA_DV
        sg_ref[rows, :] = jax.nn.silu(p[:, c0:c0 + GLA_DV]).astype(BF16)
        c0 += GLA_DV
        pool_ref[rows, :] = p[:, c0:c0 + POOL_W]
        c0 += POOL_W
        conv_ref[rows, :] = p[:, c0:c0 + CONV_W] * jax.nn.sigmoid(p[:, c0 + CONV_W:c0 + 2 * CONV_W])

    _pipelined(x_ref.shape[0], pre, main)


def _chunk_tri():
    i = np.arange(ROWS)[:, None]
    j = np.arange(ROWS)[None, :]
    same = (i // CHUNK) == (j // CHUNK)
    return jnp.asarray(np.stack([same & (j <= i), same & (j >= i)]).astype(np.float32), BF16)


def _inproj(x, mods, gains, proj_w, *, layer, tile, tiles_per_batch, fixed_row):
    n, d = x.shape
    row = _row_of_tile(tiles_per_batch, fixed_row)
    tok = lambda width: pl.BlockSpec((tile, width), lambda t: (t, 0))
    tri = _chunk_tri()
    return pl.pallas_call(
        _inproj_kernel,
        grid=(n // tile,),
        in_specs=[tok(d),
                  pl.BlockSpec((None, None, N_MOD, d), lambda t: (layer, row(t), 0, 0)),
                  _layer_spec(gains, layer)] + [_layer_spec(w, layer) for w in proj_w] + [_const_spec(tri.shape)],
        out_specs=[tok(GLA_DK), tok(GLA_DK), tok(GLA_DV), tok(GLA_DV),
                   pl.BlockSpec((2, tile, GLA_DK), lambda t: (0, t, 0)),
                   tok(POOL_W), tok(CONV_W)],
        out_shape=[jax.ShapeDtypeStruct((n, GLA_DK), BF16),
                   jax.ShapeDtypeStruct((n, GLA_DK), BF16),
                   jax.ShapeDtypeStruct((n, GLA_DV), BF16),
                   jax.ShapeDtypeStruct((n, GLA_DV), BF16),
                   jax.ShapeDtypeStruct((2, n, GLA_DK), F32),
                   jax.ShapeDtypeStruct((n, POOL_W), F32),
                   jax.ShapeDtypeStruct((n, CONV_W), F32)],
        compiler_params=_cparams(("parallel",)),
        name="inproj",
    )(x, mods, gains, *proj_w, tri)


_NT = (((1,), (1,)), ((), ()))
_TN = (((0,), (0,)), ((), ()))


def _gla_direction(q_ref, k_ref, v_ref, b_ref, o_ref, state, *, bwd):
    n_chunks = q_ref.shape[0] // CHUNK
    lane_lo = lax.broadcasted_iota(jnp.int32, (1, LANES), 1) < GLA_HEAD_K
    row = lax.broadcasted_iota(jnp.int32, (CHUNK, CHUNK), 0)
    col = lax.broadcasted_iota(jnp.int32, (CHUNK, CHUNK), 1)
    tri = (col >= row) if bwd else (col <= row)
    tri2 = jnp.concatenate([tri, tri], axis=0)

    def stack(t):
        return jnp.concatenate([jnp.where(lane_lo, t, 0.0), jnp.where(lane_lo, 0.0, t)], axis=0).astype(BF16)

    local = []
    for c in range(n_chunks):
        rows = slice(c * CHUNK, (c + 1) * CHUNK)
        qc = q_ref[rows, :].astype(F32)
        kc = k_ref[rows, :].astype(F32)
        vc = v_ref[rows, :]
        bc = b_ref[rows, :]
        b_mid = bc[CHUNK // 2:CHUNK // 2 + 1, :]
        b_tot = bc[0:1, :] if bwd else bc[CHUNK - 1:CHUNK, :]
        q_rel = qc * jnp.exp2(bc - b_mid)
        k_rel = (kc * jnp.exp2(b_mid - bc)).astype(BF16)
        q_in = qc * jnp.exp2(bc)
        k_end = kc * jnp.exp2(b_tot - bc)
        decay = jnp.exp2(b_tot)
        per_tile = []
        for t in range(GLA_DK // LANES):
            ls = slice(t * LANES, (t + 1) * LANES)
            att = lax.dot_general(stack(q_rel[:, ls]), k_rel[:, ls], _NT, preferred_element_type=F32)
            att = jnp.where(tri2, att, 0.0).astype(BF16)
            v_pair = [vc[:, (2 * t + hh) * GLA_HEAD_V:(2 * t + hh + 1) * GLA_HEAD_V] for hh in range(2)]
            o_intra = [_dot(att[hh * CHUNK:(hh + 1) * CHUNK, :], v_pair[hh]) for hh in range(2)]
            upd = lax.dot_general(jnp.concatenate(v_pair, axis=0), stack(k_end[:, ls]), _TN,
                                  preferred_element_type=F32)
            per_tile.append((stack(q_in[:, ls]), o_intra, upd, decay[:, ls]))
        local.append(per_tile)

    for c in (reversed(range(n_chunks)) if bwd else range(n_chunks)):
        outs = []
        for t, (q_in_st, o_intra, upd, decay) in enumerate(local[c]):
            o_inter = lax.dot_general(q_in_st, state[t].astype(BF16), _NT, preferred_element_type=F32)
            outs += [o_intra[hh] + o_inter[hh * CHUNK:(hh + 1) * CHUNK, :] for hh in range(2)]
            state[t] = state[t] * decay + upd
        o_ref[c * CHUNK:(c + 1) * CHUNK, :] = jnp.concatenate(outs, axis=1).astype(o_ref.dtype)
    return state


def _gla_kernel(qf_ref, kf_ref, vf_ref, bf_ref, qb_ref, kb_ref, vb_ref, bb_ref, s0_ref,
                of_ref, ob_ref, s_ref):
    @pl.when(pl.program_id(1) == 0)
    def _():
        s_ref[...] = s0_ref[...]

    n_lane_tiles = GLA_DK // LANES
    dirs = ((qf_ref, kf_ref, vf_ref, bf_ref, of_ref), (qb_ref, kb_ref, vb_ref, bb_ref, ob_ref))
    for d, refs in enumerate(dirs):
        state = [s_ref[d, :, t * LANES:(t + 1) * LANES] for t in range(n_lane_tiles)]
        state = _gla_direction(*refs, state, bwd=(d == 1))
        for t in range(n_lane_tiles):
            s_ref[d, :, t * LANES:(t + 1) * LANES] = state[t]


def _gla(q, k, v, b, s0, *, batch, tile):
    n = q.shape[0]
    nt = n // batch // tile
    fwd = lambda i, t: i * nt + t
    bwd = lambda i, t: i * nt + nt - 1 - t
    tok = lambda width, at: pl.BlockSpec((tile, width), lambda i, t: (at(i, t), 0))
    dec = lambda d, at: pl.BlockSpec((None, tile, GLA_DK), lambda i, t: (d, at(i, t), 0))
    state = pl.BlockSpec((None, 2, GLA_HEAD_V, GLA_DK), lambda i, t: (i, 0, 0, 0))
    return pl.pallas_call(
        _gla_kernel,
        grid=(batch, nt),
        in_specs=[tok(GLA_DK, fwd), tok(GLA_DK, fwd), tok(GLA_DV, fwd), dec(0, fwd),
                  tok(GLA_DK, bwd), tok(GLA_DK, bwd), tok(GLA_DV, bwd), dec(1, bwd),
                  state],
        out_specs=[tok(GLA_DV, fwd), tok(GLA_DV, bwd), state],
        out_shape=[jax.ShapeDtypeStruct((n, GLA_DV), BF16),
                   jax.ShapeDtypeStruct((n, GLA_DV), BF16),
                   jax.ShapeDtypeStruct((batch, 2, GLA_HEAD_V, GLA_DK), F32)],
        compiler_params=_cparams(("parallel", "arbitrary")),
        name="gla",
    )(q, k, v, b, q, k, v, b, s0)


def _window_sums(x, pos, period, halves, unit):
    n = x.shape[0]
    lane_hi = lax.broadcasted_iota(jnp.int32, (1, LANES), 1) >= POOL_GROUP
    trail = {1: x}
    lead = {1: x}
    w = 1
    while w < max(halves):
        trail[2 * w] = trail[w] + jnp.where(pos >= w, pltpu.roll(trail[w], w * unit, axis=0), 0.0)
        lead[2 * w] = lead[w] + jnp.where(pos < period - w, pltpu.roll(lead[w], n - w * unit, axis=0), 0.0)
        w *= 2
    t_sel = jnp.where(lane_hi, trail[halves[1]], trail[halves[0]])
    l_sel = jnp.where(lane_hi, lead[halves[1]], lead[halves[0]])
    return jnp.where(pos >= 1, pltpu.roll(t_sel, unit, axis=0), 0.0) + l_sel


def _pool_inv_count(n_rows, n_cols):
    out = np.empty((n_rows * n_cols, POOL_W), np.float32)
    r = np.arange(n_rows)[:, None]
    c = np.arange(n_cols)[None, :]
    for g, h in enumerate(POOL_HALF):
        in_rows = np.minimum(r + h, n_rows) - np.maximum(r - h, 0)
        in_cols = np.minimum(c + h, n_cols) - np.maximum(c - h, 0)
        out[:, g * POOL_GROUP:(g + 1) * POOL_GROUP] = (1.0 / (in_rows * in_cols)).reshape(-1, 1)
    return jnp.asarray(out)


def _poolconv_kernel(pool_ref, conv_ref, inv_ref, pw_ref, ps_ref, dw_ref, dwb_ref, lng_ref, lnb_ref, cpw_ref, cpb_ref,
                     o_ref, hpad_ref, *, grid2d):
    n = pool_ref.shape[0]
    x = pool_ref[...]
    tok = lax.broadcasted_iota(jnp.int32, (n, 1), 0)
    if grid2d:
        n_rows = n // GRID_W
        colp = tok % GRID_W
        rowp = tok // GRID_W
        s = [_window_sums(x[:, g * LANES:(g + 1) * LANES], rowp, n_rows, POOL_HALF[2 * g:2 * g + 2], GRID_W)
             for g in range(POOL_W // LANES)]
        s = [_window_sums(s[g], colp, GRID_W, POOL_HALF[2 * g:2 * g + 2], 1) for g in range(POOL_W // LANES)]
    else:
        s = [_window_sums(x[:, g * LANES:(g + 1) * LANES], tok, n, POOL_HALF[2 * g:2 * g + 2], 1)
             for g in range(POOL_W // LANES)]
    m = jnp.concatenate(s, axis=1) * inv_ref[...]
    y_pool = _dot((m - x).astype(BF16), pw_ref[...]) * ps_ref[...]
    o_ref[:, :POOL_W] = y_pool.astype(o_ref.dtype)

    pad = 16
    hpad_ref[0:pad, :] = jnp.zeros((pad, CONV_W), F32)
    hpad_ref[pad + n:pad + n + pad, :] = jnp.zeros((pad, CONV_W), F32)
    hpad_ref[pad:pad + n, :] = conv_ref[...]
    big = hpad_ref[...]
    total = n + 2 * pad
    acc = jnp.zeros((n, CONV_W), F32)
    for sft in range(8):
        shifted = big if sft == 0 else pltpu.roll(big, total - sft, axis=0)
        for a in range(4):
            off = 8 * a + sft
            if off < 1 or off > CONV_K:
                continue
            acc = acc + shifted[8 * a:8 * a + n, :] * dw_ref[off - 1:off, :]
    hc = acc + dwb_ref[...]
    mu = jnp.mean(hc, axis=-1, keepdims=True)
    var = jnp.mean(jnp.square(hc - mu), axis=-1, keepdims=True)
    hn = (hc - mu) * lax.rsqrt(var + EPS) * lng_ref[...] + lnb_ref[...]
    y_conv = _dot(jax.nn.silu(hn).astype(BF16), cpw_ref[...]) + cpb_ref[...]
    o_ref[:, POOL_W:] = y_conv.astype(o_ref.dtype)


def _poolconv(pool_in, conv_in, weights, *, layer, seq, grid2d):
    n = pool_in.shape[0]
    tok = lambda width: pl.BlockSpec((seq, width), lambda b: (b, 0))
    inv = _pool_inv_count(seq // GRID_W, GRID_W) if grid2d else _pool_inv_count(1, seq)
    return pl.pallas_call(
        functools.partial(_poolconv_kernel, grid2d=grid2d),
        grid=(n // seq,),
        in_specs=[tok(POOL_W), tok(CONV_W), _const_spec(inv.shape)] + [_layer_spec(w, layer) for w in weights],
        out_specs=tok(POOL_W + CONV_W),
        out_shape=jax.ShapeDtypeStruct((n, POOL_W + CONV_W), BF16),
        scratch_shapes=[pltpu.VMEM((seq + 32, CONV_W), F32)],
        compiler_params=_cparams(("parallel",)),
        name="poolconv",
    )(pool_in, conv_in, inv, *weights)


def _mixffn_kernel(x_ref, of_ref, ob_ref, sg_ref, pc_ref, mod_ref, gain_ref, gng_ref, wout_ref, wup_ref, wdn_ref,
                   out_ref):
    def pre(rows):
        o = of_ref[rows, :].astype(F32) + ob_ref[rows, :].astype(F32)
        sg = sg_ref[rows, :].astype(F32)
        ys = []
        for h in range(GLA_HEADS):
            hs = slice(h * GLA_HEAD_V, (h + 1) * GLA_HEAD_V)
            ys.append((_rms(o[:, hs], gng_ref[...]) * sg[:, hs]).astype(BF16))
        y_gla = jnp.concatenate(ys, axis=1)
        y = _dot(y_gla, wout_ref[:GLA_DV, :]) + _dot(pc_ref[rows, :], wout_ref[GLA_DV:, :])
        return _ffn_pre(_sub_out(x_ref[rows, :], y, mod_ref, gain_ref, 1, 1.0), mod_ref, gain_ref, 2)

    def main(rows, x, h):
        out_ref[rows, :] = _ffn_main(x, h, mod_ref, gain_ref, wup_ref, wdn_ref, 2, 0.5)

    _pipelined(x_ref.shape[0], pre, main)


def _mixffn(x, o_f, o_b, sg, pc, mods, gains, gng, w_out, w_up, w_dn, *, layer, tile, tiles_per_batch, fixed_row):
    n, d = x.shape
    row = _row_of_tile(tiles_per_batch, fixed_row)
    tok = lambda width: pl.BlockSpec((tile, width), lambda t: (t, 0))
    return pl.pallas_call(
        _mixffn_kernel,
        grid=(n // tile,),
        in_specs=[tok(d), tok(GLA_DV), tok(GLA_DV), tok(GLA_DV), tok(POOL_W + CONV_W),
                  pl.BlockSpec((None, None, N_MOD, d), lambda t: (layer, row(t), 0, 0)),
                  _layer_spec(gains, layer),
                  _layer_spec(gng, layer),
                  _layer_spec(w_out, layer),
                  _layer_spec(w_up, layer),
                  _layer_spec(w_dn, layer)],
        out_specs=tok(d),
        out_shape=jax.ShapeDtypeStruct((n, d), F32),
        compiler_params=_cparams(("parallel",)),
        name="mixffn",
    )(x, o_f, o_b, sg, pc, mods, gains, gng, w_out, w_up, w_dn)


def _block_diag(blocks):
    n_l, g, a, b = blocks.shape
    rows = [jnp.concatenate([blocks[:, i] if j == i else jnp.zeros((n_l, a, b), blocks.dtype) for j in range(g)], axis=2)
            for i in range(g)]
    return jnp.concatenate(rows, axis=1)


def _pick_tile(n, want):
    t = min(n, want)
    assert n % t == 0 and t % ROWS == 0
    return t


def kernel(x, c, ctx, c_ctx, w_ada, b_ada, norm_g, ffn1_up, ffn1_down, ffn2_up, ffn2_down, w_in, w_gk2, b_gk,
           gla_norm_g, pool_w, pool_scale, conv_dw, conv_dw_b, conv_ln_g, conv_ln_b, conv_pw, conv_pw_b, w_out):
    batch, seq, d = x.shape
    ctx_len = ctx.shape[1]
    depth = w_ada.shape[0]
    assert seq % GRID_W == 0 and seq % CHUNK == 0 and ctx_len % CHUNK == 0

    xf = x.reshape(batch * seq, d)
    cf = ctx.reshape(batch * ctx_len, d)

    ctx_row = batch
    n_rows = -(-(batch + 1) // 8) * 8
    cc = jnp.zeros((n_rows, d), F32).at[:batch].set(c).at[ctx_row].set(c_ctx)
    mods = _mods(cc, w_ada, b_ada).reshape(depth, n_rows, N_MOD, d)

    lat_tile = _pick_tile(seq, TOKEN_TILE)
    ctx_tile = _pick_tile(batch * ctx_len, TOKEN_TILE)
    lat = dict(tile=lat_tile, tiles_per_batch=seq // lat_tile, fixed_row=None)
    cx = dict(tile=ctx_tile, tiles_per_batch=None, fixed_row=ctx_row)
    gla_tile = _pick_tile(seq, TOKEN_TILE)

    up1, dn1 = ffn1_up.astype(BF16), ffn1_down.astype(BF16)
    up2, dn2 = ffn2_up.astype(BF16), ffn2_down.astype(BF16)
    wo = w_out.astype(BF16)
    o_lr = 2 * GLA_DK + 2 * GLA_DV
    o_pool = o_lr + 2 * GLA_LOWRANK
    w_in_p = jnp.concatenate(
        [w_in[:, :, :o_lr], w_in[:, :, o_pool:], w_in[:, :, o_lr:o_pool],
         jnp.zeros((depth, d, LR_PAD - 2 * GLA_LOWRANK), F32)], axis=2).astype(BF16)
    no_mix = jnp.zeros((depth, GLA_LOWRANK, GLA_DK), F32)
    wgk_p = jnp.concatenate(
        [jnp.concatenate([w_gk2[:, 0], no_mix], axis=2), jnp.concatenate([no_mix, w_gk2[:, 1]], axis=2),
         jnp.zeros((depth, LR_PAD - 2 * GLA_LOWRANK, 2 * GLA_DK), F32)], axis=1).astype(BF16)
    proj_w = (w_in_p, wgk_p, b_gk.reshape(depth, 1, 2 * GLA_DK))
    row = lambda a: a.reshape(depth, 1, a.shape[-1])
    pc_weights = (_block_diag(pool_w).astype(BF16), row(pool_scale),
                  jnp.pad(conv_dw, ((0, 0), (0, 1), (0, 0))), row(conv_dw_b), row(conv_ln_g), row(conv_ln_b),
                  conv_pw.astype(BF16), row(conv_pw_b))
    gng = row(gla_norm_g)
    gains = norm_g
    zero = jnp.zeros((batch, 2, GLA_HEAD_V, GLA_DK), F32)

    for l in range(depth):
        last = l == depth - 1

        xf = _ffn(xf, mods, gains, up1, dn1, layer=l, sub=0, weight=0.5, **lat)
        cf = _ffn(cf, mods, gains, up1, dn1, layer=l, sub=0, weight=0.5, **cx)
        q_c, k_c, v_c, sg_c, b_c, pool_c, conv_c = _inproj(cf, mods, gains, proj_w, layer=l, **cx)
        q_l, k_l, v_l, sg_l, b_l, pool_l, conv_l = _inproj(xf, mods, gains, proj_w, layer=l, **lat)
        of_c, ob_c, s_c = _gla(q_c, k_c, v_c, b_c, zero, batch=batch, tile=ctx_len)
        of_l, ob_l, _ = _gla(q_l, k_l, v_l, b_l, s_c, batch=batch, tile=gla_tile)
        pc_l = _poolconv(pool_l, conv_l, pc_weights, layer=l, seq=seq, grid2d=True)
        xf = _mixffn(xf, of_l, ob_l, sg_l, pc_l, mods, gains, gng, wo, up2, dn2, layer=l, **lat)
        if not last:
            pc_c = _poolconv(pool_c, conv_c, pc_weights, layer=l, seq=ctx_len, grid2d=False)
            cf = _mixffn(cf, of_c, ob_c, sg_c, pc_c, mods, gains, gng, wo, up2, dn2, layer=l, **cx)
    return xf.reshape(batch, seq, d)
```

```python
import functools
import math

import numpy as np
import jax
import jax.numpy as jnp
from jax import lax
from jax.experimental import pallas as pl
from jax.experimental.pallas import tpu as pltpu

EPS = 1e-6
GRID_W = 64
N_MOD = 9
GLA_HEADS = 4
GLA_HEAD_K = 64
GLA_HEAD_V = 128
GLA_DK = GLA_HEADS * GLA_HEAD_K
GLA_DV = GLA_HEADS * GLA_HEAD_V
QKV_W = 2 * GLA_DK + GLA_DV
GLA_LOWRANK = 16
GATE_NORM = 16.0
CHUNK = 64
LANES = 128
ROWS = 256
TOKEN_TILE = 1024
GLA_TILE = 2048
POOL_W = 256
POOL_HALF = (1, 2, 4, 8)
POOL_GROUP = 64
CONV_W = 256
CONV_K = 31
LR_PAD = 128
VMEM_LIMIT = 56 * 1024 * 1024

BF16 = jnp.bfloat16
F32 = jnp.float32


def _cparams(sem):
    return pltpu.CompilerParams(dimension_semantics=sem, vmem_limit_bytes=VMEM_LIMIT)


def _const_spec(shape):
    nd = len(shape)
    return pl.BlockSpec(shape, lambda *_: (0,) * nd, pipeline_mode=pl.Buffered(1))


def _layer_spec(arr, layer):
    nd = arr.ndim - 1
    return pl.BlockSpec((None,) + arr.shape[1:], lambda *_: (layer,) + (0,) * nd, pipeline_mode=pl.Buffered(1))


def _rms(x, g):
    return x * lax.rsqrt(jnp.mean(x * x, axis=-1, keepdims=True) + EPS) * g


def _sub_in(x, mod_ref, gain_ref, i):
    shift = mod_ref[3 * i:3 * i + 1, :]
    scale = mod_ref[3 * i + 1:3 * i + 2, :]
    return _rms(x, gain_ref[2 * i:2 * i + 1, :]) * (1.0 + scale) + shift


def _sub_out(x, y, mod_ref, gain_ref, i, weight):
    gate = mod_ref[3 * i + 2:3 * i + 3, :]
    return x + weight * gate * _rms(y, gain_ref[2 * i + 1:2 * i + 2, :])


def _dot(a, b):
    return jnp.dot(a, b, preferred_element_type=F32)


def _row_of_tile(tiles_per_batch, fixed_row):
    if fixed_row is not None:
        return lambda t: fixed_row
    return lambda t: t // tiles_per_batch


def _row_groups(n):
    return [slice(r, r + ROWS) for r in range(0, n, ROWS)]


def _pipelined(n, pre, main):
    groups = _row_groups(n)
    staged = [pre(groups[0])]
    for i, rows in enumerate(groups):
        if i + 1 < len(groups):
            staged.append(pre(groups[i + 1]))
        main(rows, *staged[i])


def _mods_kernel(c_ref, w_ref, b_ref, o_ref):
    s = jax.nn.silu(c_ref[...]).astype(BF16)
    o_ref[...] = _dot(s, w_ref[...].astype(BF16)) + b_ref[...]


def _mods(cc, w_ada, b_ada):
    depth, d, nd = w_ada.shape
    rows = cc.shape[0]
    n_col = nd // d
    return pl.pallas_call(
        _mods_kernel,
        grid=(depth, n_col),
        in_specs=[pl.BlockSpec((rows, d), lambda l, j: (0, 0)),
                  pl.BlockSpec((None, d, d), lambda l, j: (l, 0, j)),
                  pl.BlockSpec((None, 1, d), lambda l, j: (l, 0, j))],
        out_specs=pl.BlockSpec((None, rows, d), lambda l, j: (l, 0, j)),
        out_shape=jax.ShapeDtypeStruct((depth, rows, nd), F32),
        compiler_params=_cparams(("arbitrary", "arbitrary")),
        name="mods",
    )(cc, w_ada, b_ada.reshape(depth, 1, nd))


def _ffn_pre(x, mod_ref, gain_ref, sub):
    return x, _sub_in(x, mod_ref, gain_ref, sub).astype(BF16)


def _ffn_main(x, h, mod_ref, gain_ref, wup_ref, wdn_ref, sub, weight):
    d_ff = wdn_ref.shape[0]
    a = _dot(h, wup_ref[:, :d_ff])
    b = _dot(h, wup_ref[:, d_ff:])
    hh = (jax.nn.silu(a) * b).astype(BF16)
    y = _dot(hh, wdn_ref[...])
    return _sub_out(x, y, mod_ref, gain_ref, sub, weight)


def _ffn_kernel(x_ref, mod_ref, gain_ref, wup_ref, wdn_ref, o_ref, *, sub, weight):
    def pre(rows):
        return _ffn_pre(x_ref[rows, :], mod_ref, gain_ref, sub)

    def main(rows, x, h):
        o_ref[rows, :] = _ffn_main(x, h, mod_ref, gain_ref, wup_ref, wdn_ref, sub, weight)

    _pipelined(x_ref.shape[0], pre, main)


def _ffn(x, mods, gains, w_up, w_dn, *, layer, sub, weight, tile, tiles_per_batch, fixed_row):
    n, d = x.shape
    row = _row_of_tile(tiles_per_batch, fixed_row)
    return pl.pallas_call(
        functools.partial(_ffn_kernel, sub=sub, weight=weight),
        grid=(n // tile,),
        in_specs=[pl.BlockSpec((tile, d), lambda t: (t, 0)),
                  pl.BlockSpec((None, None, N_MOD, d), lambda t: (layer, row(t), 0, 0)),
                  _layer_spec(gains, layer),
                  _layer_spec(w_up, layer),
                  _layer_spec(w_dn, layer)],
        out_specs=pl.BlockSpec((tile, d), lambda t: (t, 0)),
        out_shape=jax.ShapeDtypeStruct((n, d), F32),
        compiler_params=_cparams(("parallel",)),
        name=f"ffn{sub}",
    )(x, mods, gains, w_up, w_dn)


def _log_sigmoid(x):
    return jnp.minimum(x, 0.0) - jnp.log(1.0 + jnp.exp(-jnp.abs(x)))


def _inproj_kernel(x_ref, mod_ref, gain_ref, w_ref, wgk_ref, bgk_ref, tri_ref,
                   qkv_ref, sg_ref, b_ref, pc_ref):
    n_main = w_ref.shape[1] - LR_PAD

    def pre(rows):
        return (_sub_in(x_ref[rows, :], mod_ref, gain_ref, 1).astype(BF16),)

    def main(rows, h):
        lr = _dot(h, w_ref[:, n_main:]).astype(BF16)
        logit = _dot(lr, wgk_ref[...]) + bgk_ref[...]
        lg = _log_sigmoid(logit) * (math.log2(math.e) / GATE_NORM)
        hi = lg.astype(BF16)
        lo = (lg - hi.astype(F32)).astype(BF16)
        p = _dot(h, w_ref[:, :n_main])
        for d in range(2):
            cols = slice(d * GLA_DK, (d + 1) * GLA_DK)
            b_ref[d, rows, :] = _dot(tri_ref[d], hi[:, cols]) + _dot(tri_ref[d], lo[:, cols])
        qkv_ref[rows, :GLA_DK] = (p[:, :GLA_DK] * (GLA_HEAD_K ** -0.5)).astype(BF16)
        c0 = 2 * GLA_DK + GLA_DV
        qkv_ref[rows, GLA_DK:] = p[:, GLA_DK:c0].astype(BF16)
        sg_ref[rows, :] = jax.nn.silu(p[:, c0:c0 + GLA_DV]).astype(BF16)
        c0 += GLA_DV
        pc_ref[rows, :POOL_W] = p[:, c0:c0 + POOL_W]
        c0 += POOL_W
        pc_ref[rows, POOL_W:] = p[:, c0:c0 + CONV_W] * jax.nn.sigmoid(p[:, c0 + CONV_W:c0 + 2 * CONV_W])

    _pipelined(x_ref.shape[0], pre, main)


def _chunk_tri():
    i = np.arange(ROWS)[:, None]
    j = np.arange(ROWS)[None, :]
    same = (i // CHUNK) == (j // CHUNK)
    return jnp.asarray(np.stack([same & (j <= i), same & (j >= i)]).astype(np.float32), BF16)


def _inproj(x, mods, gains, proj_w, *, layer, tile, tiles_per_batch, fixed_row):
    n, d = x.shape
    row = _row_of_tile(tiles_per_batch, fixed_row)
    tok = lambda width: pl.BlockSpec((tile, width), lambda t: (t, 0))
    tri = _chunk_tri()
    return pl.pallas_call(
        _inproj_kernel,
        grid=(n // tile,),
        in_specs=[tok(d),
                  pl.BlockSpec((None, None, N_MOD, d), lambda t: (layer, row(t), 0, 0)),
                  _layer_spec(gains, layer)] + [_layer_spec(w, layer) for w in proj_w] + [_const_spec(tri.shape)],
        out_specs=[tok(QKV_W), tok(GLA_DV),
                   pl.BlockSpec((2, tile, GLA_DK), lambda t: (0, t, 0)),
                   tok(POOL_W + CONV_W)],
        out_shape=[jax.ShapeDtypeStruct((n, QKV_W), BF16),
                   jax.ShapeDtypeStruct((n, GLA_DV), BF16),
                   jax.ShapeDtypeStruct((2, n, GLA_DK), F32),
                   jax.ShapeDtypeStruct((n, POOL_W + CONV_W), F32)],
        compiler_params=_cparams(("parallel",)),
        name="inproj",
    )(x, mods, gains, *proj_w, tri)


_NT = (((1,), (1,)), ((), ()))
_TN = (((0,), (0,)), ((), ()))


def _gla_direction(qkv_ref, b_ref, o_ref, state, *, bwd):
    n_chunks = qkv_ref.shape[0] // CHUNK
    lane_lo = lax.broadcasted_iota(jnp.int32, (1, LANES), 1) < GLA_HEAD_K
    row = lax.broadcasted_iota(jnp.int32, (CHUNK, CHUNK), 0)
    col = lax.broadcasted_iota(jnp.int32, (CHUNK, CHUNK), 1)
    tri = (col >= row) if bwd else (col <= row)
    tri2 = jnp.concatenate([tri, tri], axis=0)

    def stack(t):
        return jnp.concatenate([jnp.where(lane_lo, t, 0.0), jnp.where(lane_lo, 0.0, t)], axis=0).astype(BF16)

    local = []
    for c in range(n_chunks):
        rows = slice(c * CHUNK, (c + 1) * CHUNK)
        qc = qkv_ref[rows, :GLA_DK].astype(F32)
        kc = qkv_ref[rows, GLA_DK:2 * GLA_DK].astype(F32)
        vc = qkv_ref[rows, 2 * GLA_DK:]
        bc = b_ref[rows, :]
        b_mid = bc[CHUNK // 2:CHUNK // 2 + 1, :]
        b_tot = bc[0:1, :] if bwd else bc[CHUNK - 1:CHUNK, :]
        q_rel = qc * jnp.exp2(bc - b_mid)
        k_rel = (kc * jnp.exp2(b_mid - bc)).astype(BF16)
        q_in = qc * jnp.exp2(bc)
        k_end = kc * jnp.exp2(b_tot - bc)
        decay = jnp.exp2(b_tot)
        per_tile = []
        for t in range(GLA_DK // LANES):
            ls = slice(t * LANES, (t + 1) * LANES)
            att = lax.dot_general(stack(q_rel[:, ls]), k_rel[:, ls], _NT, preferred_element_type=F32)
            att = jnp.where(tri2, att, 0.0).astype(BF16)
            v_pair = [vc[:, (2 * t + hh) * GLA_HEAD_V:(2 * t + hh + 1) * GLA_HEAD_V] for hh in range(2)]
            o_intra = [_dot(att[hh * CHUNK:(hh + 1) * CHUNK, :], v_pair[hh]) for hh in range(2)]
            upd = lax.dot_general(jnp.concatenate(v_pair, axis=0), stack(k_end[:, ls]), _TN,
                                  preferred_element_type=F32)
            per_tile.append((stack(q_in[:, ls]), o_intra, upd, decay[:, ls]))
        local.append(per_tile)

    for c in (reversed(range(n_chunks)) if bwd else range(n_chunks)):
        outs = []
        for t, (q_in_st, o_intra, upd, decay) in enumerate(local[c]):
            o_inter = lax.dot_general(q_in_st, state[t].astype(BF16), _NT, preferred_element_type=F32)
            outs += [o_intra[hh] + o_inter[hh * CHUNK:(hh + 1) * CHUNK, :] for hh in range(2)]
            state[t] = state[t] * decay + upd
        o_ref[c * CHUNK:(c + 1) * CHUNK, :] = jnp.concatenate(outs, axis=1).astype(o_ref.dtype)
    return state


def _gla_kernel(qkvf_ref, bf_ref, qkvb_ref, bb_ref, s0_ref, of_ref, ob_ref, s_ref):
    @pl.when(pl.program_id(1) == 0)
    def _():
        s_ref[...] = s0_ref[...]

    n_lane_tiles = GLA_DK // LANES
    dirs = ((qkvf_ref, bf_ref, of_ref), (qkvb_ref, bb_ref, ob_ref))
    for d, refs in enumerate(dirs):
        state = [s_ref[d, :, t * LANES:(t + 1) * LANES] for t in range(n_lane_tiles)]
        state = _gla_direction(*refs, state, bwd=(d == 1))
        for t in range(n_lane_tiles):
            s_ref[d, :, t * LANES:(t + 1) * LANES] = state[t]


def _gla(qkv, b, s0, *, batch, tile):
    n = qkv.shape[0]
    nt = n // batch // tile
    fwd = lambda i, t: i * nt + t
    bwd = lambda i, t: i * nt + nt - 1 - t
    tok = lambda width, at: pl.BlockSpec((tile, width), lambda i, t: (at(i, t), 0))
    dec = lambda d, at: pl.BlockSpec((None, tile, GLA_DK), lambda i, t: (d, at(i, t), 0))
    state = pl.BlockSpec((None, 2, GLA_HEAD_V, GLA_DK), lambda i, t: (i, 0, 0, 0))
    return pl.pallas_call(
        _gla_kernel,
        grid=(batch, nt),
        in_specs=[tok(QKV_W, fwd), dec(0, fwd), tok(QKV_W, bwd), dec(1, bwd), state],
        out_specs=[tok(GLA_DV, fwd), tok(GLA_DV, bwd), state],
        out_shape=[jax.ShapeDtypeStruct((n, GLA_DV), BF16),
                   jax.ShapeDtypeStruct((n, GLA_DV), BF16),
                   jax.ShapeDtypeStruct((batch, 2, GLA_HEAD_V, GLA_DK), F32)],
        compiler_params=_cparams(("parallel", "arbitrary")),
        name="gla",
    )(qkv, b, qkv, b, s0)


def _window_sums(x, pos, period, halves, unit):
    n = x.shape[0]
    lane_hi = lax.broadcasted_iota(jnp.int32, (1, LANES), 1) >= POOL_GROUP
    trail = {1: x}
    lead = {1: x}
    w = 1
    while w < max(halves):
        trail[2 * w] = trail[w] + jnp.where(pos >= w, pltpu.roll(trail[w], w * unit, axis=0), 0.0)
        lead[2 * w] = lead[w] + jnp.where(pos < period - w, pltpu.roll(lead[w], n - w * unit, axis=0), 0.0)
        w *= 2
    t_sel = jnp.where(lane_hi, trail[halves[1]], trail[halves[0]])
    l_sel = jnp.where(lane_hi, lead[halves[1]], lead[halves[0]])
    return jnp.where(pos >= 1, pltpu.roll(t_sel, unit, axis=0), 0.0) + l_sel


def _pool_inv_count(n_rows, n_cols):
    out = np.empty((n_rows * n_cols, POOL_W), np.float32)
    r = np.arange(n_rows)[:, None]
    c = np.arange(n_cols)[None, :]
    for g, h in enumerate(POOL_HALF):
        in_rows = np.minimum(r + h, n_rows) - np.maximum(r - h, 0)
        in_cols = np.minimum(c + h, n_cols) - np.maximum(c - h, 0)
        out[:, g * POOL_GROUP:(g + 1) * POOL_GROUP] = (1.0 / (in_rows * in_cols)).reshape(-1, 1)
    return jnp.asarray(out)


def _poolconv_kernel(in_ref, inv_ref, pw_ref, ps_ref, dw_ref, dwb_ref, lng_ref, lnb_ref, cpw_ref, cpb_ref,
                     o_ref, hpad_ref, *, grid2d):
    n = in_ref.shape[0]
    x = in_ref[:, :POOL_W]
    tok = lax.broadcasted_iota(jnp.int32, (n, 1), 0)
    if grid2d:
        n_rows = n // GRID_W
        colp = tok % GRID_W
        rowp = tok // GRID_W
        s = [_window_sums(x[:, g * LANES:(g + 1) * LANES], rowp, n_rows, POOL_HALF[2 * g:2 * g + 2], GRID_W)
             for g in range(POOL_W // LANES)]
        s = [_window_sums(s[g], colp, GRID_W, POOL_HALF[2 * g:2 * g + 2], 1) for g in range(POOL_W // LANES)]
    else:
        s = [_window_sums(x[:, g * LANES:(g + 1) * LANES], tok, n, POOL_HALF[2 * g:2 * g + 2], 1)
             for g in range(POOL_W // LANES)]
    m = jnp.concatenate(s, axis=1) * inv_ref[...]
    y_pool = _dot((m - x).astype(BF16), pw_ref[...]) * ps_ref[...]
    o_ref[:, :POOL_W] = y_pool.astype(o_ref.dtype)

    pad = 16
    hpad_ref[0:pad, :] = jnp.zeros((pad, CONV_W), F32)
    hpad_ref[pad + n:pad + n + pad, :] = jnp.zeros((pad, CONV_W), F32)
    hpad_ref[pad:pad + n, :] = in_ref[:, POOL_W:]
    big = hpad_ref[...]
    total = n + 2 * pad
    acc = jnp.zeros((n, CONV_W), F32)
    for sft in range(8):
        shifted = big if sft == 0 else pltpu.roll(big, total - sft, axis=0)
        for a in range(4):
            off = 8 * a + sft
            if off < 1 or off > CONV_K:
                continue
            acc = acc + shifted[8 * a:8 * a + n, :] * dw_ref[off - 1:off, :]
    hc = acc + dwb_ref[...]
    mu = jnp.mean(hc, axis=-1, keepdims=True)
    var = jnp.mean(jnp.square(hc - mu), axis=-1, keepdims=True)
    hn = (hc - mu) * lax.rsqrt(var + EPS) * lng_ref[...] + lnb_ref[...]
    y_conv = _dot(jax.nn.silu(hn).astype(BF16), cpw_ref[...]) + cpb_ref[...]
    o_ref[:, POOL_W:] = y_conv.astype(o_ref.dtype)


def _poolconv(pc_in, weights, *, layer, seq, grid2d):
    n = pc_in.shape[0]
    tok = lambda width: pl.BlockSpec((seq, width), lambda b: (b, 0))
    inv = _pool_inv_count(seq // GRID_W, GRID_W) if grid2d else _pool_inv_count(1, seq)
    return pl.pallas_call(
        functools.partial(_poolconv_kernel, grid2d=grid2d),
        grid=(n // seq,),
        in_specs=[tok(POOL_W + CONV_W), _const_spec(inv.shape)] + [_layer_spec(w, layer) for w in weights],
        out_specs=tok(POOL_W + CONV_W),
        out_shape=jax.ShapeDtypeStruct((n, POOL_W + CONV_W), BF16),
        scratch_shapes=[pltpu.VMEM((seq + 32, CONV_W), F32)],
        compiler_params=_cparams(("parallel",)),
        name="poolconv",
    )(pc_in, inv, *weights)


def _mixffn_kernel(x_ref, of_ref, ob_ref, sg_ref, pc_ref, mod_ref, gain_ref, gng_ref, wout_ref, wup_ref, wdn_ref,
                   out_ref):
    def pre(rows):
        o = of_ref[rows, :].astype(F32) + ob_ref[rows, :].astype(F32)
        sg = sg_ref[rows, :].astype(F32)
        ys = []
        for h in range(GLA_HEADS):
            hs = slice(h * GLA_HEAD_V, (h + 1) * GLA_HEAD_V)
            ys.append((_rms(o[:, hs], gng_ref[...]) * sg[:, hs]).astype(BF16))
        y_gla = jnp.concatenate(ys, axis=1)
        y = _dot(y_gla, wout_ref[:GLA_DV, :]) + _dot(pc_ref[rows, :], wout_ref[GLA_DV:, :])
        return _ffn_pre(_sub_out(x_ref[rows, :], y, mod_ref, gain_ref, 1, 1.0), mod_ref, gain_ref, 2)

    def main(rows, x, h):
        out_ref[rows, :] = _ffn_main(x, h, mod_ref, gain_ref, wup_ref, wdn_ref, 2, 0.5)

    _pipelined(x_ref.shape[0], pre, main)


def _mixffn(x, o_f, o_b, sg, pc, mods, gains, gng, w_out, w_up, w_dn, *, layer, tile, tiles_per_batch, fixed_row):
    n, d = x.shape
    row = _row_of_tile(tiles_per_batch, fixed_row)
    tok = lambda width: pl.BlockSpec((tile, width), lambda t: (t, 0))
    return pl.pallas_call(
        _mixffn_kernel,
        grid=(n // tile,),
        in_specs=[tok(d), tok(GLA_DV), tok(GLA_DV), tok(GLA_DV), tok(POOL_W + CONV_W),
                  pl.BlockSpec((None, None, N_MOD, d), lambda t: (layer, row(t), 0, 0)),
                  _layer_spec(gains, layer),
                  _layer_spec(gng, layer),
                  _layer_spec(w_out, layer),
                  _layer_spec(w_up, layer),
                  _layer_spec(w_dn, layer)],
        out_specs=tok(d),
        out_shape=jax.ShapeDtypeStruct((n, d), F32),
        compiler_params=_cparams(("parallel",)),
        name="mixffn",
    )(x, o_f, o_b, sg, pc, mods, gains, gng, w_out, w_up, w_dn)


def _block_diag(blocks):
    n_l, g, a, b = blocks.shape
    rows = [jnp.concatenate([blocks[:, i] if j == i else jnp.zeros((n_l, a, b), blocks.dtype) for j in range(g)], axis=2)
            for i in range(g)]
    return jnp.concatenate(rows, axis=1)


def _pick_tile(n, want):
    t = min(n, want)
    assert n % t == 0 and t % ROWS == 0
    return t


def kernel(x, c, ctx, c_ctx, w_ada, b_ada, norm_g, ffn1_up, ffn1_down, ffn2_up, ffn2_down, w_in, w_gk2, b_gk,
           gla_norm_g, pool_w, pool_scale, conv_dw, conv_dw_b, conv_ln_g, conv_ln_b, conv_pw, conv_pw_b, w_out):
    batch, seq, d = x.shape
    ctx_len = ctx.shape[1]
    depth = w_ada.shape[0]
    assert seq % GRID_W == 0 and seq % CHUNK == 0 and ctx_len % CHUNK == 0

    xf = x.reshape(batch * seq, d)
    cf = ctx.reshape(batch * ctx_len, d)

    ctx_row = batch
    n_rows = -(-(batch + 1) // 8) * 8
    cc = jnp.zeros((n_rows, d), F32).at[:batch].set(c).at[ctx_row].set(c_ctx)
    mods = _mods(cc, w_ada, b_ada).reshape(depth, n_rows, N_MOD, d)

    lat_tile = _pick_tile(seq, TOKEN_TILE)
    ctx_tile = _pick_tile(batch * ctx_len, TOKEN_TILE)
    lat = dict(tile=lat_tile, tiles_per_batch=seq // lat_tile, fixed_row=None)
    cx = dict(tile=ctx_tile, tiles_per_batch=None, fixed_row=ctx_row)
    gla_tile = _pick_tile(seq, GLA_TILE)

    up1, dn1 = ffn1_up.astype(BF16), ffn1_down.astype(BF16)
    up2, dn2 = ffn2_up.astype(BF16), ffn2_down.astype(BF16)
    wo = w_out.astype(BF16)
    o_lr = 2 * GLA_DK + 2 * GLA_DV
    o_pool = o_lr + 2 * GLA_LOWRANK
    w_in_p = jnp.concatenate(
        [w_in[:, :, :o_lr], w_in[:, :, o_pool:], w_in[:, :, o_lr:o_pool],
         jnp.zeros((depth, d, LR_PAD - 2 * GLA_LOWRANK), F32)], axis=2).astype(BF16)
    no_mix = jnp.zeros((depth, GLA_LOWRANK, GLA_DK), F32)
    wgk_p = jnp.concatenate(
        [jnp.concatenate([w_gk2[:, 0], no_mix], axis=2), jnp.concatenate([no_mix, w_gk2[:, 1]], axis=2),
         jnp.zeros((depth, LR_PAD - 2 * GLA_LOWRANK, 2 * GLA_DK), F32)], axis=1).astype(BF16)
    proj_w = (w_in_p, wgk_p, b_gk.reshape(depth, 1, 2 * GLA_DK))
    row = lambda a: a.reshape(depth, 1, a.shape[-1])
    pc_weights = (_block_diag(pool_w).astype(BF16), row(pool_scale),
                  jnp.pad(conv_dw, ((0, 0), (0, 1), (0, 0))), row(conv_dw_b), row(conv_ln_g), row(conv_ln_b),
                  conv_pw.astype(BF16), row(conv_pw_b))
    gng = row(gla_norm_g)
    gains = norm_g
    zero = jnp.zeros((batch, 2, GLA_HEAD_V, GLA_DK), F32)

    for l in range(depth):
        last = l == depth - 1

        xf = _ffn(xf, mods, gains, up1, dn1, layer=l, sub=0, weight=0.5, **lat)
        cf = _ffn(cf, mods, gains, up1, dn1, layer=l, sub=0, weight=0.5, **cx)
        qkv_c, sg_c, b_c, pcin_c = _inproj(cf, mods, gains, proj_w, layer=l, **cx)
        qkv_l, sg_l, b_l, pcin_l = _inproj(xf, mods, gains, proj_w, layer=l, **lat)
        of_c, ob_c, s_c = _gla(qkv_c, b_c, zero, batch=batch, tile=ctx_len)
        of_l, ob_l, _ = _gla(qkv_l, b_l, s_c, batch=batch, tile=gla_tile)
        pc_l = _poolconv(pcin_l, pc_weights, layer=l, seq=seq, grid2d=True)
        xf = _mixffn(xf, of_l, ob_l, sg_l, pc_l, mods, gains, gng, wo, up2, dn2, layer=l, **lat)
        if not last:
            pc_c = _poolconv(pcin_c, pc_weights, layer=l, seq=ctx_len, grid2d=False)
            cf = _mixffn(cf, of_c, ob_c, sg_c, pc_c, mods, gains, gng, wo, up2, dn2, layer=l, **cx)
    return xf.reshape(batch, seq, d)
```

```python
import functools
import math

import numpy as np
import jax
import jax.numpy as jnp
from jax import lax
from jax.experimental import pallas as pl
from jax.experimental.pallas import tpu as pltpu

EPS = 1e-6
GRID_W = 64
N_MOD = 9
GLA_HEADS = 4
GLA_HEAD_K = 64
GLA_HEAD_V = 128
GLA_DK = GLA_HEADS * GLA_HEAD_K
GLA_DV = GLA_HEADS * GLA_HEAD_V
QKV_W = 2 * GLA_DK + GLA_DV
GLA_LOWRANK = 16
GATE_NORM = 16.0
CHUNK = 64
LANES = 128
ROWS = 256
TOKEN_TILE = 1024
PROJ_TILE = 2048
GLA_TILE = 2048
POOL_W = 256
POOL_HALF = (1, 2, 4, 8)
POOL_GROUP = 64
CONV_W = 256
CONV_K = 31
CONV_PAD = 16
LR_PAD = 128
VMEM_LIMIT = 56 * 1024 * 1024

BF16 = jnp.bfloat16
F32 = jnp.float32


def _cparams(sem):
    return pltpu.CompilerParams(dimension_semantics=sem, vmem_limit_bytes=VMEM_LIMIT)


def _const_spec(shape):
    nd = len(shape)
    return pl.BlockSpec(shape, lambda *_: (0,) * nd, pipeline_mode=pl.Buffered(1))


def _layer_spec(arr, layer):
    nd = arr.ndim - 1
    return pl.BlockSpec((None,) + arr.shape[1:], lambda *_: (layer,) + (0,) * nd, pipeline_mode=pl.Buffered(1))


def _rms(x, g):
    return x * lax.rsqrt(jnp.mean(x * x, axis=-1, keepdims=True) + EPS) * g


def _sub_in(x, mod_ref, gain_ref, i):
    shift = mod_ref[3 * i:3 * i + 1, :]
    scale = mod_ref[3 * i + 1:3 * i + 2, :]
    return _rms(x, gain_ref[2 * i:2 * i + 1, :]) * (1.0 + scale) + shift


def _sub_out(x, y, mod_ref, gain_ref, i, weight):
    gate = mod_ref[3 * i + 2:3 * i + 3, :]
    return x + weight * gate * _rms(y, gain_ref[2 * i + 1:2 * i + 2, :])


def _dot(a, b):
    return jnp.dot(a, b, preferred_element_type=F32)


def _row_of_tile(tiles_per_batch, fixed_row):
    if fixed_row is not None:
        return lambda t: fixed_row
    return lambda t: t // tiles_per_batch


def _row_groups(n):
    return [slice(r, r + ROWS) for r in range(0, n, ROWS)]


def _pipelined(n, pre, main):
    groups = _row_groups(n)
    staged = [pre(groups[0])]
    for i, rows in enumerate(groups):
        if i + 1 < len(groups):
            staged.append(pre(groups[i + 1]))
        main(rows, *staged[i])


def _mods_kernel(c_ref, w_ref, b_ref, o_ref):
    s = jax.nn.silu(c_ref[...]).astype(BF16)
    o_ref[...] = _dot(s, w_ref[...].astype(BF16)) + b_ref[...]


def _mods(cc, w_ada, b_ada):
    depth, d, nd = w_ada.shape
    rows = cc.shape[0]
    n_col = nd // d
    return pl.pallas_call(
        _mods_kernel,
        grid=(depth, n_col),
        in_specs=[pl.BlockSpec((rows, d), lambda l, j: (0, 0)),
                  pl.BlockSpec((None, d, d), lambda l, j: (l, 0, j)),
                  pl.BlockSpec((None, 1, d), lambda l, j: (l, 0, j))],
        out_specs=pl.BlockSpec((None, rows, d), lambda l, j: (l, 0, j)),
        out_shape=jax.ShapeDtypeStruct((depth, rows, nd), F32),
        compiler_params=_cparams(("arbitrary", "arbitrary")),
        name="mods",
    )(cc, w_ada, b_ada.reshape(depth, 1, nd))


def _ffn_pre(x, mod_ref, gain_ref, sub):
    return x, _sub_in(x, mod_ref, gain_ref, sub).astype(BF16)


def _ffn_main(x, h, mod_ref, gain_ref, wup_ref, wdn_ref, sub, weight):
    d_ff = wdn_ref.shape[0]
    a = _dot(h, wup_ref[:, :d_ff])
    b = _dot(h, wup_ref[:, d_ff:])
    hh = (jax.nn.silu(a) * b).astype(BF16)
    y = _dot(hh, wdn_ref[...])
    return _sub_out(x, y, mod_ref, gain_ref, sub, weight)


def _ffn_kernel(x_ref, mod_ref, gain_ref, wup_ref, wdn_ref, o_ref, *, sub, weight):
    def pre(rows):
        return _ffn_pre(x_ref[rows, :], mod_ref, gain_ref, sub)

    def main(rows, x, h):
        o_ref[rows, :] = _ffn_main(x, h, mod_ref, gain_ref, wup_ref, wdn_ref, sub, weight)

    _pipelined(x_ref.shape[0], pre, main)


def _ffn(x, mods, gains, w_up, w_dn, *, layer, sub, weight, tile, tiles_per_batch, fixed_row):
    n, d = x.shape
    row = _row_of_tile(tiles_per_batch, fixed_row)
    return pl.pallas_call(
        functools.partial(_ffn_kernel, sub=sub, weight=weight),
        grid=(n // tile,),
        in_specs=[pl.BlockSpec((tile, d), lambda t: (t, 0)),
                  pl.BlockSpec((None, None, N_MOD, d), lambda t: (layer, row(t), 0, 0)),
                  _layer_spec(gains, layer),
                  _layer_spec(w_up, layer),
                  _layer_spec(w_dn, layer)],
        out_specs=pl.BlockSpec((tile, d), lambda t: (t, 0)),
        out_shape=jax.ShapeDtypeStruct((n, d), F32),
        compiler_params=_cparams(("parallel",)),
        name=f"ffn{sub}",
    )(x, mods, gains, w_up, w_dn)


def _log_sigmoid(x):
    return jnp.minimum(x, 0.0) - jnp.log(1.0 + jnp.exp(-jnp.abs(x)))


def _inproj_kernel(x_ref, mod_ref, gain_ref, w_ref, wgk_ref, bgk_ref, tri_ref,
                   qkv_ref, sg_ref, b_ref, pc_ref):
    n_main = w_ref.shape[1] - LR_PAD

    def pre(rows):
        return (_sub_in(x_ref[rows, :], mod_ref, gain_ref, 1).astype(BF16),)

    def main(rows, h):
        lr = _dot(h, w_ref[:, n_main:]).astype(BF16)
        logit = _dot(lr, wgk_ref[...]) + bgk_ref[...]
        lg = _log_sigmoid(logit) * (math.log2(math.e) / GATE_NORM)
        hi = lg.astype(BF16)
        lo = (lg - hi.astype(F32)).astype(BF16)
        p = _dot(h, w_ref[:, :n_main])
        for d in range(2):
            cols = slice(d * GLA_DK, (d + 1) * GLA_DK)
            b_ref[d, rows, :] = _dot(tri_ref[d], hi[:, cols]) + _dot(tri_ref[d], lo[:, cols])
        qkv_ref[rows, :GLA_DK] = (p[:, :GLA_DK] * (GLA_HEAD_K ** -0.5)).astype(BF16)
        c0 = 2 * GLA_DK + GLA_DV
        qkv_ref[rows, GLA_DK:] = p[:, GLA_DK:c0].astype(BF16)
        sg_ref[rows, :] = jax.nn.silu(p[:, c0:c0 + GLA_DV]).astype(BF16)
        c0 += GLA_DV
        pc_ref[rows, :POOL_W] = p[:, c0:c0 + POOL_W]
        c0 += POOL_W
        pc_ref[rows, POOL_W:] = p[:, c0:c0 + CONV_W] * jax.nn.sigmoid(p[:, c0 + CONV_W:c0 + 2 * CONV_W])

    _pipelined(x_ref.shape[0], pre, main)


def _chunk_tri():
    i = np.arange(ROWS)[:, None]
    j = np.arange(ROWS)[None, :]
    same = (i // CHUNK) == (j // CHUNK)
    return jnp.asarray(np.stack([same & (j <= i), same & (j >= i)]).astype(np.float32), BF16)


def _inproj(x, mods, gains, proj_w, *, layer, tile, tiles_per_batch, fixed_row):
    n, d = x.shape
    row = _row_of_tile(tiles_per_batch, fixed_row)
    tok = lambda width: pl.BlockSpec((tile, width), lambda t: (t, 0))
    tri = _chunk_tri()
    return pl.pallas_call(
        _inproj_kernel,
        grid=(n // tile,),
        in_specs=[tok(d),
                  pl.BlockSpec((None, None, N_MOD, d), lambda t: (layer, row(t), 0, 0)),
                  _layer_spec(gains, layer)] + [_layer_spec(w, layer) for w in proj_w] + [_const_spec(tri.shape)],
        out_specs=[tok(QKV_W), tok(GLA_DV),
                   pl.BlockSpec((2, tile, GLA_DK), lambda t: (0, t, 0)),
                   tok(POOL_W + CONV_W)],
        out_shape=[jax.ShapeDtypeStruct((n, QKV_W), BF16),
                   jax.ShapeDtypeStruct((n, GLA_DV), BF16),
                   jax.ShapeDtypeStruct((2, n, GLA_DK), F32),
                   jax.ShapeDtypeStruct((n, POOL_W + CONV_W), F32)],
        compiler_params=_cparams(("parallel",)),
        name="inproj",
    )(x, mods, gains, *proj_w, tri)


_NT = (((1,), (1,)), ((), ()))
_TN = (((0,), (0,)), ((), ()))


def _gla_direction(qkv_ref, b_ref, o_ref, state, *, bwd):
    n_chunks = qkv_ref.shape[0] // CHUNK
    lane_lo = lax.broadcasted_iota(jnp.int32, (1, LANES), 1) < GLA_HEAD_K
    row = lax.broadcasted_iota(jnp.int32, (CHUNK, CHUNK), 0)
    col = lax.broadcasted_iota(jnp.int32, (CHUNK, CHUNK), 1)
    tri = (col >= row) if bwd else (col <= row)
    tri2 = jnp.concatenate([tri, tri], axis=0)

    def stack(t):
        return jnp.concatenate([jnp.where(lane_lo, t, 0.0), jnp.where(lane_lo, 0.0, t)], axis=0).astype(BF16)

    local = []
    for c in range(n_chunks):
        rows = slice(c * CHUNK, (c + 1) * CHUNK)
        qc = qkv_ref[rows, :GLA_DK].astype(F32)
        kc = qkv_ref[rows, GLA_DK:2 * GLA_DK].astype(F32)
        vc = qkv_ref[rows, 2 * GLA_DK:]
        bc = b_ref[rows, :]
        b_mid = bc[CHUNK // 2:CHUNK // 2 + 1, :]
        b_tot = bc[0:1, :] if bwd else bc[CHUNK - 1:CHUNK, :]
        q_rel = qc * jnp.exp2(bc - b_mid)
        k_rel = (kc * jnp.exp2(b_mid - bc)).astype(BF16)
        q_in = qc * jnp.exp2(bc)
        k_end = kc * jnp.exp2(b_tot - bc)
        decay = jnp.exp2(b_tot)
        per_tile = []
        for t in range(GLA_DK // LANES):
            ls = slice(t * LANES, (t + 1) * LANES)
            att = lax.dot_general(stack(q_rel[:, ls]), k_rel[:, ls], _NT, preferred_element_type=F32)
            att = jnp.where(tri2, att, 0.0).astype(BF16)
            v_pair = [vc[:, (2 * t + hh) * GLA_HEAD_V:(2 * t + hh + 1) * GLA_HEAD_V] for hh in range(2)]
            o_intra = [_dot(att[hh * CHUNK:(hh + 1) * CHUNK, :], v_pair[hh]) for hh in range(2)]
            upd = lax.dot_general(jnp.concatenate(v_pair, axis=0), stack(k_end[:, ls]), _TN,
                                  preferred_element_type=F32)
            per_tile.append((stack(q_in[:, ls]), o_intra, upd, decay[:, ls]))
        local.append(per_tile)

    for c in (reversed(range(n_chunks)) if bwd else range(n_chunks)):
        outs = []
        for t, (q_in_st, o_intra, upd, decay) in enumerate(local[c]):
            o_inter = lax.dot_general(q_in_st, state[t].astype(BF16), _NT, preferred_element_type=F32)
            outs += [o_intra[hh] + o_inter[hh * CHUNK:(hh + 1) * CHUNK, :] for hh in range(2)]
            state[t] = state[t] * decay + upd
        o_ref[c * CHUNK:(c + 1) * CHUNK, :] = jnp.concatenate(outs, axis=1).astype(o_ref.dtype)
    return state


def _gla_kernel(qkvf_ref, bf_ref, qkvb_ref, bb_ref, s0_ref, of_ref, ob_ref, s_ref):
    @pl.when(pl.program_id(1) == 0)
    def _():
        s_ref[...] = s0_ref[...]

    n_lane_tiles = GLA_DK // LANES
    dirs = ((qkvf_ref, bf_ref, of_ref), (qkvb_ref, bb_ref, ob_ref))
    for d, refs in enumerate(dirs):
        state = [s_ref[d, :, t * LANES:(t + 1) * LANES] for t in range(n_lane_tiles)]
        state = _gla_direction(*refs, state, bwd=(d == 1))
        for t in range(n_lane_tiles):
            s_ref[d, :, t * LANES:(t + 1) * LANES] = state[t]


def _gla(qkv, b, s0, *, batch, tile):
    n = qkv.shape[0]
    nt = n // batch // tile
    fwd = lambda i, t: i * nt + t
    bwd = lambda i, t: i * nt + nt - 1 - t
    tok = lambda width, at: pl.BlockSpec((tile, width), lambda i, t: (at(i, t), 0))
    dec = lambda d, at: pl.BlockSpec((None, tile, GLA_DK), lambda i, t: (d, at(i, t), 0))
    state = pl.BlockSpec((None, 2, GLA_HEAD_V, GLA_DK), lambda i, t: (i, 0, 0, 0))
    return pl.pallas_call(
        _gla_kernel,
        grid=(batch, nt),
        in_specs=[tok(QKV_W, fwd), dec(0, fwd), tok(QKV_W, bwd), dec(1, bwd), state],
        out_specs=[tok(GLA_DV, fwd), tok(GLA_DV, bwd), state],
        out_shape=[jax.ShapeDtypeStruct((n, GLA_DV), BF16),
                   jax.ShapeDtypeStruct((n, GLA_DV), BF16),
                   jax.ShapeDtypeStruct((batch, 2, GLA_HEAD_V, GLA_DK), F32)],
        compiler_params=_cparams(("parallel", "arbitrary")),
        name="gla",
    )(qkv, b, qkv, b, s0)


def _window_sums(x, pos, period, halves, unit):
    n = x.shape[0]
    lane_hi = lax.broadcasted_iota(jnp.int32, (1, LANES), 1) >= POOL_GROUP
    trail = {1: x}
    lead = {1: x}
    w = 1
    while w < max(halves):
        trail[2 * w] = trail[w] + jnp.where(pos >= w, pltpu.roll(trail[w], w * unit, axis=0), 0.0)
        lead[2 * w] = lead[w] + jnp.where(pos < period - w, pltpu.roll(lead[w], n - w * unit, axis=0), 0.0)
        w *= 2
    t_sel = jnp.where(lane_hi, trail[halves[1]], trail[halves[0]])
    l_sel = jnp.where(lane_hi, lead[halves[1]], lead[halves[0]])
    return jnp.where(pos >= 1, pltpu.roll(t_sel, unit, axis=0), 0.0) + l_sel


def _pool_inv_count(n_rows, n_cols):
    out = np.empty((n_rows * n_cols, POOL_W), np.float32)
    r = np.arange(n_rows)[:, None]
    c = np.arange(n_cols)[None, :]
    for g, h in enumerate(POOL_HALF):
        in_rows = np.minimum(r + h, n_rows) - np.maximum(r - h, 0)
        in_cols = np.minimum(c + h, n_cols) - np.maximum(c - h, 0)
        out[:, g * POOL_GROUP:(g + 1) * POOL_GROUP] = (1.0 / (in_rows * in_cols)).reshape(-1, 1)
    return jnp.asarray(out)


def _poolconv_kernel(in_ref, inv_ref, pw_ref, ps_ref, dw_ref, dwb_ref, lng_ref, lnb_ref, cpw_ref, cpb_ref,
                     o_ref, hpad_ref, *, grid2d):
    n = in_ref.shape[0]
    x = in_ref[:, :POOL_W]
    tok = lax.broadcasted_iota(jnp.int32, (n, 1), 0)
    if grid2d:
        n_rows = n // GRID_W
        colp = tok % GRID_W
        rowp = tok // GRID_W
        s = [_window_sums(x[:, g * LANES:(g + 1) * LANES], rowp, n_rows, POOL_HALF[2 * g:2 * g + 2], GRID_W)
             for g in range(POOL_W // LANES)]
        s = [_window_sums(s[g], colp, GRID_W, POOL_HALF[2 * g:2 * g + 2], 1) for g in range(POOL_W // LANES)]
    else:
        s = [_window_sums(x[:, g * LANES:(g + 1) * LANES], tok, n, POOL_HALF[2 * g:2 * g + 2], 1)
             for g in range(POOL_W // LANES)]
    m = jnp.concatenate(s, axis=1) * inv_ref[...]
    y_pool = _dot((m - x).astype(BF16), pw_ref[...]) * ps_ref[...]
    o_ref[:, :POOL_W] = y_pool.astype(o_ref.dtype)

    pad = CONV_PAD
    hpad_ref[0:pad, :] = jnp.zeros((pad, CONV_W), F32)
    hpad_ref[pad + n:pad + n + pad, :] = jnp.zeros((pad, CONV_W), F32)
    hpad_ref[pad:pad + n, :] = in_ref[:, POOL_W:]
    big = hpad_ref[...]
    total = n + 2 * pad
    acc = jnp.zeros((n, CONV_W), F32)
    for sft in range(8):
        shifted = big if sft == 0 else pltpu.roll(big, total - sft, axis=0)
        for a in range(4):
            off = 8 * a + sft
            if off < 1 or off > CONV_K:
                continue
            acc = acc + shifted[8 * a:8 * a + n, :] * dw_ref[off - 1:off, :]
    hc = acc + dwb_ref[...]
    mu = jnp.mean(hc, axis=-1, keepdims=True)
    var = jnp.mean(jnp.square(hc - mu), axis=-1, keepdims=True)
    hn = (hc - mu) * lax.rsqrt(var + EPS) * lng_ref[...] + lnb_ref[...]
    y_conv = _dot(jax.nn.silu(hn).astype(BF16), cpw_ref[...]) + cpb_ref[...]
    o_ref[:, POOL_W:] = y_conv.astype(o_ref.dtype)


def _poolconv(pc_in, weights, *, layer, seq, grid2d):
    n = pc_in.shape[0]
    tok = lambda width: pl.BlockSpec((seq, width), lambda b: (b, 0))
    inv = _pool_inv_count(seq // GRID_W, GRID_W) if grid2d else _pool_inv_count(1, seq)
    return pl.pallas_call(
        functools.partial(_poolconv_kernel, grid2d=grid2d),
        grid=(n // seq,),
        in_specs=[tok(POOL_W + CONV_W), _const_spec(inv.shape)] + [_layer_spec(w, layer) for w in weights],
        out_specs=tok(POOL_W + CONV_W),
        out_shape=jax.ShapeDtypeStruct((n, POOL_W + CONV_W), BF16),
        scratch_shapes=[pltpu.VMEM((seq + 2 * CONV_PAD, CONV_W), F32)],
        compiler_params=_cparams(("parallel",)),
        name="poolconv",
    )(pc_in, inv, *weights)


def _mixffn_kernel(x_ref, of_ref, ob_ref, sg_ref, pc_ref, mod_ref, gain_ref, gng_ref, wout_ref, wup_ref, wdn_ref,
                   out_ref):
    def pre(rows):
        o = of_ref[rows, :].astype(F32) + ob_ref[rows, :].astype(F32)
        sg = sg_ref[rows, :].astype(F32)
        ys = []
        for h in range(GLA_HEADS):
            hs = slice(h * GLA_HEAD_V, (h + 1) * GLA_HEAD_V)
            ys.append((_rms(o[:, hs], gng_ref[...]) * sg[:, hs]).astype(BF16))
        y_gla = jnp.concatenate(ys, axis=1)
        y = _dot(y_gla, wout_ref[:GLA_DV, :]) + _dot(pc_ref[rows, :], wout_ref[GLA_DV:, :])
        return _ffn_pre(_sub_out(x_ref[rows, :], y, mod_ref, gain_ref, 1, 1.0), mod_ref, gain_ref, 2)

    def main(rows, x, h):
        out_ref[rows, :] = _ffn_main(x, h, mod_ref, gain_ref, wup_ref, wdn_ref, 2, 0.5)

    _pipelined(x_ref.shape[0], pre, main)


def _mixffn(x, o_f, o_b, sg, pc, mods, gains, gng, w_out, w_up, w_dn, *, layer, tile, tiles_per_batch, fixed_row):
    n, d = x.shape
    row = _row_of_tile(tiles_per_batch, fixed_row)
    tok = lambda width: pl.BlockSpec((tile, width), lambda t: (t, 0))
    return pl.pallas_call(
        _mixffn_kernel,
        grid=(n // tile,),
        in_specs=[tok(d), tok(GLA_DV), tok(GLA_DV), tok(GLA_DV), tok(POOL_W + CONV_W),
                  pl.BlockSpec((None, None, N_MOD, d), lambda t: (layer, row(t), 0, 0)),
                  _layer_spec(gains, layer),
                  _layer_spec(gng, layer),
                  _layer_spec(w_out, layer),
                  _layer_spec(w_up, layer),
                  _layer_spec(w_dn, layer)],
        out_specs=tok(d),
        out_shape=jax.ShapeDtypeStruct((n, d), F32),
        compiler_params=_cparams(("parallel",)),
        name="mixffn",
    )(x, o_f, o_b, sg, pc, mods, gains, gng, w_out, w_up, w_dn)


def _block_diag(blocks):
    n_l, g, a, b = blocks.shape
    rows = [jnp.concatenate([blocks[:, i] if j == i else jnp.zeros((n_l, a, b), blocks.dtype) for j in range(g)], axis=2)
            for i in range(g)]
    return jnp.concatenate(rows, axis=1)


def _pick_tile(n, want):
    t = min(n, want)
    assert n % t == 0 and t % ROWS == 0
    return t


def kernel(x, c, ctx, c_ctx, w_ada, b_ada, norm_g, ffn1_up, ffn1_down, ffn2_up, ffn2_down, w_in, w_gk2, b_gk,
           gla_norm_g, pool_w, pool_scale, conv_dw, conv_dw_b, conv_ln_g, conv_ln_b, conv_pw, conv_pw_b, w_out):
    batch, seq, d = x.shape
    ctx_len = ctx.shape[1]
    depth = w_ada.shape[0]
    assert seq % GRID_W == 0 and seq % CHUNK == 0 and ctx_len % CHUNK == 0

    xf = x.reshape(batch * seq, d)
    cf = ctx.reshape(batch * ctx_len, d)

    ctx_row = batch
    n_rows = -(-(batch + 1) // 8) * 8
    cc = jnp.zeros((n_rows, d), F32).at[:batch].set(c).at[ctx_row].set(c_ctx)
    mods = _mods(cc, w_ada, b_ada).reshape(depth, n_rows, N_MOD, d)

    lat_tile = _pick_tile(seq, TOKEN_TILE)
    ctx_tile = _pick_tile(batch * ctx_len, TOKEN_TILE)
    lat = dict(tile=lat_tile, tiles_per_batch=seq // lat_tile, fixed_row=None)
    cx = dict(tile=ctx_tile, tiles_per_batch=None, fixed_row=ctx_row)
    proj_tile = _pick_tile(seq, PROJ_TILE)
    lat_proj = dict(tile=proj_tile, tiles_per_batch=seq // proj_tile, fixed_row=None)
    cx_proj = dict(cx, tile=_pick_tile(batch * ctx_len, PROJ_TILE))
    gla_tile = _pick_tile(seq, GLA_TILE)

    up1, dn1 = ffn1_up.astype(BF16), ffn1_down.astype(BF16)
    up2, dn2 = ffn2_up.astype(BF16), ffn2_down.astype(BF16)
    wo = w_out.astype(BF16)
    o_lr = 2 * GLA_DK + 2 * GLA_DV
    o_pool = o_lr + 2 * GLA_LOWRANK
    w_in_p = jnp.concatenate(
        [w_in[:, :, :o_lr], w_in[:, :, o_pool:], w_in[:, :, o_lr:o_pool],
         jnp.zeros((depth, d, LR_PAD - 2 * GLA_LOWRANK), F32)], axis=2).astype(BF16)
    no_mix = jnp.zeros((depth, GLA_LOWRANK, GLA_DK), F32)
    wgk_p = jnp.concatenate(
        [jnp.concatenate([w_gk2[:, 0], no_mix], axis=2), jnp.concatenate([no_mix, w_gk2[:, 1]], axis=2),
         jnp.zeros((depth, LR_PAD - 2 * GLA_LOWRANK, 2 * GLA_DK), F32)], axis=1).astype(BF16)
    proj_w = (w_in_p, wgk_p, b_gk.reshape(depth, 1, 2 * GLA_DK))
    row = lambda a: a.reshape(depth, 1, a.shape[-1])
    pc_weights = (_block_diag(pool_w).astype(BF16), row(pool_scale),
                  jnp.pad(conv_dw, ((0, 0), (0, 1), (0, 0))), row(conv_dw_b), row(conv_ln_g), row(conv_ln_b),
                  conv_pw.astype(BF16), row(conv_pw_b))
    gng = row(gla_norm_g)
    gains = norm_g
    zero = jnp.zeros((batch, 2, GLA_HEAD_V, GLA_DK), F32)

    for l in range(depth):
        last = l == depth - 1

        xf = _ffn(xf, mods, gains, up1, dn1, layer=l, sub=0, weight=0.5, **lat)
        cf = _ffn(cf, mods, gains, up1, dn1, layer=l, sub=0, weight=0.5, **cx)
        qkv_c, sg_c, b_c, pcin_c = _inproj(cf, mods, gains, proj_w, layer=l, **cx_proj)
        qkv_l, sg_l, b_l, pcin_l = _inproj(xf, mods, gains, proj_w, layer=l, **lat_proj)
        of_c, ob_c, s_c = _gla(qkv_c, b_c, zero, batch=batch, tile=ctx_len)
        of_l, ob_l, _ = _gla(qkv_l, b_l, s_c, batch=batch, tile=gla_tile)
        pc_l = _poolconv(pcin_l, pc_weights, layer=l, seq=seq, grid2d=True)
        xf = _mixffn(xf, of_l, ob_l, sg_l, pc_l, mods, gains, gng, wo, up2, dn2, layer=l, **lat)
        if not last:
            pc_c = _poolconv(pcin_c, pc_weights, layer=l, seq=ctx_len, grid2d=False)
            cf = _mixffn(cf, of_c, ob_c, sg_c, pc_c, mods, gains, gng, wo, up2, dn2, layer=l, **cx)
    return xf.reshape(batch, seq, d)
```

```python
import functools
import math

import numpy as np
import jax
import jax.numpy as jnp
from jax import lax
from jax.experimental import pallas as pl
from jax.experimental.pallas import tpu as pltpu

EPS = 1e-6
GRID_W = 64
N_MOD = 9
GLA_HEADS = 4
GLA_HEAD_K = 64
GLA_HEAD_V = 128
GLA_DK = GLA_HEADS * GLA_HEAD_K
GLA_DV = GLA_HEADS * GLA_HEAD_V
QKV_W = 2 * GLA_DK + GLA_DV
GLA_LOWRANK = 16
GATE_NORM = 16.0
CHUNK = 128
LANES = 128
ROWS = 256
TOKEN_TILE = 1024
PROJ_TILE = 2048
GLA_TILE = 2048
POOL_W = 256
POOL_HALF = (1, 2, 4, 8)
POOL_GROUP = 64
CONV_W = 256
CONV_K = 31
CONV_PAD = 16
LR_PAD = 128
VMEM_LIMIT = 56 * 1024 * 1024

BF16 = jnp.bfloat16
F32 = jnp.float32


def _cparams(sem):
    return pltpu.CompilerParams(dimension_semantics=sem, vmem_limit_bytes=VMEM_LIMIT)


def _const_spec(shape):
    nd = len(shape)
    return pl.BlockSpec(shape, lambda *_: (0,) * nd, pipeline_mode=pl.Buffered(1))


def _layer_spec(arr, layer):
    nd = arr.ndim - 1
    return pl.BlockSpec((None,) + arr.shape[1:], lambda *_: (layer,) + (0,) * nd, pipeline_mode=pl.Buffered(1))


def _rms(x, g):
    return x * lax.rsqrt(jnp.mean(x * x, axis=-1, keepdims=True) + EPS) * g


def _sub_in(x, mod_ref, gain_ref, i):
    shift = mod_ref[3 * i:3 * i + 1, :]
    scale = mod_ref[3 * i + 1:3 * i + 2, :]
    return _rms(x, gain_ref[2 * i:2 * i + 1, :]) * (1.0 + scale) + shift


def _sub_out(x, y, mod_ref, gain_ref, i, weight):
    gate = mod_ref[3 * i + 2:3 * i + 3, :]
    return x + weight * gate * _rms(y, gain_ref[2 * i + 1:2 * i + 2, :])


def _dot(a, b):
    return jnp.dot(a, b, preferred_element_type=F32)


def _row_of_tile(tiles_per_batch, fixed_row):
    if fixed_row is not None:
        return lambda t: fixed_row
    return lambda t: t // tiles_per_batch


def _row_groups(n):
    return [slice(r, r + ROWS) for r in range(0, n, ROWS)]


def _pipelined(n, pre, main):
    groups = _row_groups(n)
    staged = [pre(groups[0])]
    for i, rows in enumerate(groups):
        if i + 1 < len(groups):
            staged.append(pre(groups[i + 1]))
        main(rows, *staged[i])


def _mods_kernel(c_ref, w_ref, b_ref, o_ref):
    s = jax.nn.silu(c_ref[...]).astype(BF16)
    o_ref[...] = _dot(s, w_ref[...].astype(BF16)) + b_ref[...]


def _mods(cc, w_ada, b_ada):
    depth, d, nd = w_ada.shape
    rows = cc.shape[0]
    n_col = nd // d
    return pl.pallas_call(
        _mods_kernel,
        grid=(depth, n_col),
        in_specs=[pl.BlockSpec((rows, d), lambda l, j: (0, 0)),
                  pl.BlockSpec((None, d, d), lambda l, j: (l, 0, j)),
                  pl.BlockSpec((None, 1, d), lambda l, j: (l, 0, j))],
        out_specs=pl.BlockSpec((None, rows, d), lambda l, j: (l, 0, j)),
        out_shape=jax.ShapeDtypeStruct((depth, rows, nd), F32),
        compiler_params=_cparams(("arbitrary", "arbitrary")),
        name="mods",
    )(cc, w_ada, b_ada.reshape(depth, 1, nd))


def _ffn_pre(x, mod_ref, gain_ref, sub):
    return x, _sub_in(x, mod_ref, gain_ref, sub).astype(BF16)


def _ffn_main(x, h, mod_ref, gain_ref, wup_ref, wdn_ref, sub, weight):
    d_ff = wdn_ref.shape[0]
    a = _dot(h, wup_ref[:, :d_ff])
    b = _dot(h, wup_ref[:, d_ff:])
    hh = (jax.nn.silu(a) * b).astype(BF16)
    y = _dot(hh, wdn_ref[...])
    return _sub_out(x, y, mod_ref, gain_ref, sub, weight)


def _ffn_kernel(x_ref, mod_ref, gain_ref, wup_ref, wdn_ref, o_ref, *, sub, weight):
    def pre(rows):
        return _ffn_pre(x_ref[rows, :], mod_ref, gain_ref, sub)

    def main(rows, x, h):
        o_ref[rows, :] = _ffn_main(x, h, mod_ref, gain_ref, wup_ref, wdn_ref, sub, weight)

    _pipelined(x_ref.shape[0], pre, main)


def _ffn(x, mods, gains, w_up, w_dn, *, layer, sub, weight, tile, tiles_per_batch, fixed_row):
    n, d = x.shape
    row = _row_of_tile(tiles_per_batch, fixed_row)
    return pl.pallas_call(
        functools.partial(_ffn_kernel, sub=sub, weight=weight),
        grid=(n // tile,),
        in_specs=[pl.BlockSpec((tile, d), lambda t: (t, 0)),
                  pl.BlockSpec((None, None, N_MOD, d), lambda t: (layer, row(t), 0, 0)),
                  _layer_spec(gains, layer),
                  _layer_spec(w_up, layer),
                  _layer_spec(w_dn, layer)],
        out_specs=pl.BlockSpec((tile, d), lambda t: (t, 0)),
        out_shape=jax.ShapeDtypeStruct((n, d), F32),
        compiler_params=_cparams(("parallel",)),
        name=f"ffn{sub}",
    )(x, mods, gains, w_up, w_dn)


def _log_sigmoid(x):
    return jnp.minimum(x, 0.0) - jnp.log(1.0 + jnp.exp(-jnp.abs(x)))


def _inproj_kernel(x_ref, mod_ref, gain_ref, w_ref, wgk_ref, bgk_ref, tri_ref,
                   qkv_ref, sg_ref, b_ref, pc_ref):
    n_main = w_ref.shape[1] - LR_PAD

    def pre(rows):
        return (_sub_in(x_ref[rows, :], mod_ref, gain_ref, 1).astype(BF16),)

    def main(rows, h):
        lr = _dot(h, w_ref[:, n_main:]).astype(BF16)
        logit = _dot(lr, wgk_ref[...]) + bgk_ref[...]
        lg = _log_sigmoid(logit) * (math.log2(math.e) / GATE_NORM)
        hi = lg.astype(BF16)
        lo = (lg - hi.astype(F32)).astype(BF16)
        p = _dot(h, w_ref[:, :n_main])
        for d in range(2):
            cols = slice(d * GLA_DK, (d + 1) * GLA_DK)
            b_ref[d, rows, :] = _dot(tri_ref[d], hi[:, cols]) + _dot(tri_ref[d], lo[:, cols])
        qkv_ref[rows, :GLA_DK] = (p[:, :GLA_DK] * (GLA_HEAD_K ** -0.5)).astype(BF16)
        c0 = 2 * GLA_DK + GLA_DV
        qkv_ref[rows, GLA_DK:] = p[:, GLA_DK:c0].astype(BF16)
        sg_ref[rows, :] = jax.nn.silu(p[:, c0:c0 + GLA_DV]).astype(BF16)
        c0 += GLA_DV
        pc_ref[rows, :POOL_W] = p[:, c0:c0 + POOL_W]
        c0 += POOL_W
        pc_ref[rows, POOL_W:] = p[:, c0:c0 + CONV_W] * jax.nn.sigmoid(p[:, c0 + CONV_W:c0 + 2 * CONV_W])

    _pipelined(x_ref.shape[0], pre, main)


def _chunk_tri():
    i = np.arange(ROWS)[:, None]
    j = np.arange(ROWS)[None, :]
    same = (i // CHUNK) == (j // CHUNK)
    return jnp.asarray(np.stack([same & (j <= i), same & (j >= i)]).astype(np.float32), BF16)


def _inproj(x, mods, gains, proj_w, *, layer, tile, tiles_per_batch, fixed_row):
    n, d = x.shape
    row = _row_of_tile(tiles_per_batch, fixed_row)
    tok = lambda width: pl.BlockSpec((tile, width), lambda t: (t, 0))
    tri = _chunk_tri()
    return pl.pallas_call(
        _inproj_kernel,
        grid=(n // tile,),
        in_specs=[tok(d),
                  pl.BlockSpec((None, None, N_MOD, d), lambda t: (layer, row(t), 0, 0)),
                  _layer_spec(gains, layer)] + [_layer_spec(w, layer) for w in proj_w] + [_const_spec(tri.shape)],
        out_specs=[tok(QKV_W), tok(GLA_DV),
                   pl.BlockSpec((2, tile, GLA_DK), lambda t: (0, t, 0)),
                   tok(POOL_W + CONV_W)],
        out_shape=[jax.ShapeDtypeStruct((n, QKV_W), BF16),
                   jax.ShapeDtypeStruct((n, GLA_DV), BF16),
                   jax.ShapeDtypeStruct((2, n, GLA_DK), F32),
                   jax.ShapeDtypeStruct((n, POOL_W + CONV_W), F32)],
        compiler_params=_cparams(("parallel",)),
        name="inproj",
    )(x, mods, gains, *proj_w, tri)


_NT = (((1,), (1,)), ((), ()))
_TN = (((0,), (0,)), ((), ()))


def _gla_direction(qkv_ref, b_ref, o_ref, state, *, bwd):
    n_chunks = qkv_ref.shape[0] // CHUNK
    lane_lo = lax.broadcasted_iota(jnp.int32, (1, LANES), 1) < GLA_HEAD_K
    row = lax.broadcasted_iota(jnp.int32, (CHUNK, CHUNK), 0)
    col = lax.broadcasted_iota(jnp.int32, (CHUNK, CHUNK), 1)
    tri = (col >= row) if bwd else (col <= row)
    tri2 = jnp.concatenate([tri, tri], axis=0)

    def stack(t):
        return jnp.concatenate([jnp.where(lane_lo, t, 0.0), jnp.where(lane_lo, 0.0, t)], axis=0).astype(BF16)

    local = []
    for c in range(n_chunks):
        rows = slice(c * CHUNK, (c + 1) * CHUNK)
        qc = qkv_ref[rows, :GLA_DK].astype(F32)
        kc = qkv_ref[rows, GLA_DK:2 * GLA_DK].astype(F32)
        vc = qkv_ref[rows, 2 * GLA_DK:]
        bc = b_ref[rows, :]
        b_mid = bc[CHUNK // 2:CHUNK // 2 + 1, :]
        b_tot = bc[0:1, :] if bwd else bc[CHUNK - 1:CHUNK, :]
        q_rel = qc * jnp.exp2(bc - b_mid)
        k_rel = (kc * jnp.exp2(b_mid - bc)).astype(BF16)
        q_in = qc * jnp.exp2(bc)
        k_end = kc * jnp.exp2(b_tot - bc)
        decay = jnp.exp2(b_tot)
        per_tile = []
        for t in range(GLA_DK // LANES):
            ls = slice(t * LANES, (t + 1) * LANES)
            att = lax.dot_general(stack(q_rel[:, ls]), k_rel[:, ls], _NT, preferred_element_type=F32)
            att = jnp.where(tri2, att, 0.0).astype(BF16)
            v_pair = [vc[:, (2 * t + hh) * GLA_HEAD_V:(2 * t + hh + 1) * GLA_HEAD_V] for hh in range(2)]
            o_intra = [_dot(att[hh * CHUNK:(hh + 1) * CHUNK, :], v_pair[hh]) for hh in range(2)]
            upd = lax.dot_general(jnp.concatenate(v_pair, axis=0), stack(k_end[:, ls]), _TN,
                                  preferred_element_type=F32)
            per_tile.append((stack(q_in[:, ls]), o_intra, upd, decay[:, ls]))
        local.append(per_tile)

    for c in (reversed(range(n_chunks)) if bwd else range(n_chunks)):
        outs = []
        for t, (q_in_st, o_intra, upd, decay) in enumerate(local[c]):
            o_inter = lax.dot_general(q_in_st, state[t].astype(BF16), _NT, preferred_element_type=F32)
            outs += [o_intra[hh] + o_inter[hh * CHUNK:(hh + 1) * CHUNK, :] for hh in range(2)]
            state[t] = state[t] * decay + upd
        o_ref[c * CHUNK:(c + 1) * CHUNK, :] = jnp.concatenate(outs, axis=1).astype(o_ref.dtype)
    return state


def _gla_kernel(qkvf_ref, bf_ref, qkvb_ref, bb_ref, s0_ref, of_ref, ob_ref, s_ref):
    @pl.when(pl.program_id(1) == 0)
    def _():
        s_ref[...] = s0_ref[...]

    n_lane_tiles = GLA_DK // LANES
    dirs = ((qkvf_ref, bf_ref, of_ref), (qkvb_ref, bb_ref, ob_ref))
    for d, refs in enumerate(dirs):
        state = [s_ref[d, :, t * LANES:(t + 1) * LANES] for t in range(n_lane_tiles)]
        state = _gla_direction(*refs, state, bwd=(d == 1))
        for t in range(n_lane_tiles):
            s_ref[d, :, t * LANES:(t + 1) * LANES] = state[t]


def _gla(qkv, b, s0, *, batch, tile):
    n = qkv.shape[0]
    nt = n // batch // tile
    fwd = lambda i, t: i * nt + t
    bwd = lambda i, t: i * nt + nt - 1 - t
    tok = lambda width, at: pl.BlockSpec((tile, width), lambda i, t: (at(i, t), 0))
    dec = lambda d, at: pl.BlockSpec((None, tile, GLA_DK), lambda i, t: (d, at(i, t), 0))
    state = pl.BlockSpec((None, 2, GLA_HEAD_V, GLA_DK), lambda i, t: (i, 0, 0, 0))
    return pl.pallas_call(
        _gla_kernel,
        grid=(batch, nt),
        in_specs=[tok(QKV_W, fwd), dec(0, fwd), tok(QKV_W, bwd), dec(1, bwd), state],
        out_specs=[tok(GLA_DV, fwd), tok(GLA_DV, bwd), state],
        out_shape=[jax.ShapeDtypeStruct((n, GLA_DV), BF16),
                   jax.ShapeDtypeStruct((n, GLA_DV), BF16),
                   jax.ShapeDtypeStruct((batch, 2, GLA_HEAD_V, GLA_DK), F32)],
        compiler_params=_cparams(("parallel", "arbitrary")),
        name="gla",
    )(qkv, b, qkv, b, s0)


def _window_sums(x, pos, period, halves, unit):
    n = x.shape[0]
    lane_hi = lax.broadcasted_iota(jnp.int32, (1, LANES), 1) >= POOL_GROUP
    trail = {1: x}
    lead = {1: x}
    w = 1
    while w < max(halves):
        trail[2 * w] = trail[w] + jnp.where(pos >= w, pltpu.roll(trail[w], w * unit, axis=0), 0.0)
        lead[2 * w] = lead[w] + jnp.where(pos < period - w, pltpu.roll(lead[w], n - w * unit, axis=0), 0.0)
        w *= 2
    t_sel = jnp.where(lane_hi, trail[halves[1]], trail[halves[0]])
    l_sel = jnp.where(lane_hi, lead[halves[1]], lead[halves[0]])
    return jnp.where(pos >= 1, pltpu.roll(t_sel, unit, axis=0), 0.0) + l_sel


def _pool_inv_count(n_rows, n_cols):
    out = np.empty((n_rows * n_cols, POOL_W), np.float32)
    r = np.arange(n_rows)[:, None]
    c = np.arange(n_cols)[None, :]
    for g, h in enumerate(POOL_HALF):
        in_rows = np.minimum(r + h, n_rows) - np.maximum(r - h, 0)
        in_cols = np.minimum(c + h, n_cols) - np.maximum(c - h, 0)
        out[:, g * POOL_GROUP:(g + 1) * POOL_GROUP] = (1.0 / (in_rows * in_cols)).reshape(-1, 1)
    return jnp.asarray(out)


def _poolconv_kernel(in_ref, inv_ref, pw_ref, ps_ref, dw_ref, dwb_ref, lng_ref, lnb_ref, cpw_ref, cpb_ref,
                     o_ref, hpad_ref, *, grid2d):
    n = in_ref.shape[0]
    x = in_ref[:, :POOL_W]
    tok = lax.broadcasted_iota(jnp.int32, (n, 1), 0)
    if grid2d:
        n_rows = n // GRID_W
        colp = tok % GRID_W
        rowp = tok // GRID_W
        s = [_window_sums(x[:, g * LANES:(g + 1) * LANES], rowp, n_rows, POOL_HALF[2 * g:2 * g + 2], GRID_W)
             for g in range(POOL_W // LANES)]
        s = [_window_sums(s[g], colp, GRID_W, POOL_HALF[2 * g:2 * g + 2], 1) for g in range(POOL_W // LANES)]
    else:
        s = [_window_sums(x[:, g * LANES:(g + 1) * LANES], tok, n, POOL_HALF[2 * g:2 * g + 2], 1)
             for g in range(POOL_W // LANES)]
    m = jnp.concatenate(s, axis=1) * inv_ref[...]
    y_pool = _dot((m - x).astype(BF16), pw_ref[...]) * ps_ref[...]
    o_ref[:, :POOL_W] = y_pool.astype(o_ref.dtype)

    pad = CONV_PAD
    hpad_ref[0:pad, :] = jnp.zeros((pad, CONV_W), F32)
    hpad_ref[pad + n:pad + n + pad, :] = jnp.zeros((pad, CONV_W), F32)
    hpad_ref[pad:pad + n, :] = in_ref[:, POOL_W:]
    big = hpad_ref[...]
    total = n + 2 * pad
    acc = jnp.zeros((n, CONV_W), F32)
    for sft in range(8):
        shifted = big if sft == 0 else pltpu.roll(big, total - sft, axis=0)
        for a in range(4):
            off = 8 * a + sft
            if off < 1 or off > CONV_K:
                continue
            acc = acc + shifted[8 * a:8 * a + n, :] * dw_ref[off - 1:off, :]
    hc = acc + dwb_ref[...]
    mu = jnp.mean(hc, axis=-1, keepdims=True)
    var = jnp.mean(jnp.square(hc - mu), axis=-1, keepdims=True)
    hn = (hc - mu) * lax.rsqrt(var + EPS) * lng_ref[...] + lnb_ref[...]
    y_conv = _dot(jax.nn.silu(hn).astype(BF16), cpw_ref[...]) + cpb_ref[...]
    o_ref[:, POOL_W:] = y_conv.astype(o_ref.dtype)


def _poolconv(pc_in, weights, *, layer, seq, grid2d):
    n = pc_in.shape[0]
    tok = lambda width: pl.BlockSpec((seq, width), lambda b: (b, 0))
    inv = _pool_inv_count(seq // GRID_W, GRID_W) if grid2d else _pool_inv_count(1, seq)
    return pl.pallas_call(
        functools.partial(_poolconv_kernel, grid2d=grid2d),
        grid=(n // seq,),
        in_specs=[tok(POOL_W + CONV_W), _const_spec(inv.shape)] + [_layer_spec(w, layer) for w in weights],
        out_specs=tok(POOL_W + CONV_W),
        out_shape=jax.ShapeDtypeStruct((n, POOL_W + CONV_W), BF16),
        scratch_shapes=[pltpu.VMEM((seq + 2 * CONV_PAD, CONV_W), F32)],
        compiler_params=_cparams(("parallel",)),
        name="poolconv",
    )(pc_in, inv, *weights)


def _mixffn_kernel(x_ref, of_ref, ob_ref, sg_ref, pc_ref, mod_ref, gain_ref, gng_ref, wout_ref, wup_ref, wdn_ref,
                   out_ref):
    def pre(rows):
        o = of_ref[rows, :].astype(F32) + ob_ref[rows, :].astype(F32)
        sg = sg_ref[rows, :].astype(F32)
        ys = []
        for h in range(GLA_HEADS):
            hs = slice(h * GLA_HEAD_V, (h + 1) * GLA_HEAD_V)
            ys.append((_rms(o[:, hs], gng_ref[...]) * sg[:, hs]).astype(BF16))
        y_gla = jnp.concatenate(ys, axis=1)
        y = _dot(y_gla, wout_ref[:GLA_DV, :]) + _dot(pc_ref[rows, :], wout_ref[GLA_DV:, :])
        return _ffn_pre(_sub_out(x_ref[rows, :], y, mod_ref, gain_ref, 1, 1.0), mod_ref, gain_ref, 2)

    def main(rows, x, h):
        out_ref[rows, :] = _ffn_main(x, h, mod_ref, gain_ref, wup_ref, wdn_ref, 2, 0.5)

    _pipelined(x_ref.shape[0], pre, main)


def _mixffn(x, o_f, o_b, sg, pc, mods, gains, gng, w_out, w_up, w_dn, *, layer, tile, tiles_per_batch, fixed_row):
    n, d = x.shape
    row = _row_of_tile(tiles_per_batch, fixed_row)
    tok = lambda width: pl.BlockSpec((tile, width), lambda t: (t, 0))
    return pl.pallas_call(
        _mixffn_kernel,
        grid=(n // tile,),
        in_specs=[tok(d), tok(GLA_DV), tok(GLA_DV), tok(GLA_DV), tok(POOL_W + CONV_W),
                  pl.BlockSpec((None, None, N_MOD, d), lambda t: (layer, row(t), 0, 0)),
                  _layer_spec(gains, layer),
                  _layer_spec(gng, layer),
                  _layer_spec(w_out, layer),
                  _layer_spec(w_up, layer),
                  _layer_spec(w_dn, layer)],
        out_specs=tok(d),
        out_shape=jax.ShapeDtypeStruct((n, d), F32),
        compiler_params=_cparams(("parallel",)),
        name="mixffn",
    )(x, o_f, o_b, sg, pc, mods, gains, gng, w_out, w_up, w_dn)


def _block_diag(blocks):
    n_l, g, a, b = blocks.shape
    rows = [jnp.concatenate([blocks[:, i] if j == i else jnp.zeros((n_l, a, b), blocks.dtype) for j in range(g)], axis=2)
            for i in range(g)]
    return jnp.concatenate(rows, axis=1)


def _pick_tile(n, want):
    t = min(n, want)
    assert n % t == 0 and t % ROWS == 0
    return t


def kernel(x, c, ctx, c_ctx, w_ada, b_ada, norm_g, ffn1_up, ffn1_down, ffn2_up, ffn2_down, w_in, w_gk2, b_gk,
           gla_norm_g, pool_w, pool_scale, conv_dw, conv_dw_b, conv_ln_g, conv_ln_b, conv_pw, conv_pw_b, w_out):
    batch, seq, d = x.shape
    ctx_len = ctx.shape[1]
    depth = w_ada.shape[0]
    assert seq % GRID_W == 0 and seq % CHUNK == 0 and ctx_len % CHUNK == 0

    xf = x.reshape(batch * seq, d)
    cf = ctx.reshape(batch * ctx_len, d)

    ctx_row = batch
    n_rows = -(-(batch + 1) // 8) * 8
    cc = jnp.zeros((n_rows, d), F32).at[:batch].set(c).at[ctx_row].set(c_ctx)
    mods = _mods(cc, w_ada, b_ada).reshape(depth, n_rows, N_MOD, d)

    lat_tile = _pick_tile(seq, TOKEN_TILE)
    ctx_tile = _pick_tile(batch * ctx_len, TOKEN_TILE)
    lat = dict(tile=lat_tile, tiles_per_batch=seq // lat_tile, fixed_row=None)
    cx = dict(tile=ctx_tile, tiles_per_batch=None, fixed_row=ctx_row)
    proj_tile = _pick_tile(seq, PROJ_TILE)
    lat_proj = dict(tile=proj_tile, tiles_per_batch=seq // proj_tile, fixed_row=None)
    cx_proj = dict(cx, tile=_pick_tile(batch * ctx_len, PROJ_TILE))
    gla_tile = _pick_tile(seq, GLA_TILE)

    up1, dn1 = ffn1_up.astype(BF16), ffn1_down.astype(BF16)
    up2, dn2 = ffn2_up.astype(BF16), ffn2_down.astype(BF16)
    wo = w_out.astype(BF16)
    o_lr = 2 * GLA_DK + 2 * GLA_DV
    o_pool = o_lr + 2 * GLA_LOWRANK
    w_in_p = jnp.concatenate(
        [w_in[:, :, :o_lr], w_in[:, :, o_pool:], w_in[:, :, o_lr:o_pool],
         jnp.zeros((depth, d, LR_PAD - 2 * GLA_LOWRANK), F32)], axis=2).astype(BF16)
    no_mix = jnp.zeros((depth, GLA_LOWRANK, GLA_DK), F32)
    wgk_p = jnp.concatenate(
        [jnp.concatenate([w_gk2[:, 0], no_mix], axis=2), jnp.concatenate([no_mix, w_gk2[:, 1]], axis=2),
         jnp.zeros((depth, LR_PAD - 2 * GLA_LOWRANK, 2 * GLA_DK), F32)], axis=1).astype(BF16)
    proj_w = (w_in_p, wgk_p, b_gk.reshape(depth, 1, 2 * GLA_DK))
    row = lambda a: a.reshape(depth, 1, a.shape[-1])
    pc_weights = (_block_diag(pool_w).astype(BF16), row(pool_scale),
                  jnp.pad(conv_dw, ((0, 0), (0, 1), (0, 0))), row(conv_dw_b), row(conv_ln_g), row(conv_ln_b),
                  conv_pw.astype(BF16), row(conv_pw_b))
    gng = row(gla_norm_g)
    gains = norm_g
    zero = jnp.zeros((batch, 2, GLA_HEAD_V, GLA_DK), F32)

    for l in range(depth):
        last = l == depth - 1

        xf = _ffn(xf, mods, gains, up1, dn1, layer=l, sub=0, weight=0.5, **lat)
        cf = _ffn(cf, mods, gains, up1, dn1, layer=l, sub=0, weight=0.5, **cx)
        qkv_c, sg_c, b_c, pcin_c = _inproj(cf, mods, gains, proj_w, layer=l, **cx_proj)
        qkv_l, sg_l, b_l, pcin_l = _inproj(xf, mods, gains, proj_w, layer=l, **lat_proj)
        of_c, ob_c, s_c = _gla(qkv_c, b_c, zero, batch=batch, tile=ctx_len)
        of_l, ob_l, _ = _gla(qkv_l, b_l, s_c, batch=batch, tile=gla_tile)
        pc_l = _poolconv(pcin_l, pc_weights, layer=l, seq=seq, grid2d=True)
        xf = _mixffn(xf, of_l, ob_l, sg_l, pc_l, mods, gains, gng, wo, up2, dn2, layer=l, **lat)
        if not last:
            pc_c = _poolconv(pcin_c, pc_weights, layer=l, seq=ctx_len, grid2d=False)
            cf = _mixffn(cf, of_c, ob_c, sg_c, pc_c, mods, gains, gng, wo, up2, dn2, layer=l, **cx)
    return xf.reshape(batch, seq, d)
```
